```python
import jax, jax.numpy as jnp
from jax import lax
import numpy as np

D_MODEL = 1024
BATCH = 16
SEQ = 4096
DEPTH = 1
DEC_BATCH = 32
DEC_SEQ = 16
PAST_LEN = 4096

CHUNK = 64
POOL_WIDTH = D_MODEL // 1
POOL_GROUPS = 4
POOL_GROUP_DIM = POOL_WIDTH // POOL_GROUPS
POOL_WINDOWS = (2, 4, 8, 16)
POOL_HIST = max(POOL_WINDOWS) - 1
SSD_WIDTH = D_MODEL
SSD_HEAD_DIM = 64
SSD_HEADS = SSD_WIDTH // SSD_HEAD_DIM
SSD_GROUPS = 2
SSD_HEADS_PER_GROUP = SSD_HEADS // SSD_GROUPS
SSD_STATE = 128
SSD_CONV = 4
SSD_CHUNK = CHUNK
CONV_DIM = SSD_WIDTH + 2 * SSD_GROUPS * SSD_STATE
MIX_WIDTH = POOL_WIDTH + SSD_WIDTH
D_IN_PROJ = POOL_WIDTH + SSD_WIDTH + CONV_DIM + SSD_HEADS
D_FF = 4 * D_MODEL
EPS = 1e-6

kernel_name = "hymba_pool_ssd_streaming_step"


def rms_norm(x, w):
    xf = x.astype(jnp.float32)
    y = xf * lax.rsqrt(jnp.mean(xf * xf, axis=-1, keepdims=True) + EPS)
    return y.astype(x.dtype) * w


def pool_mixer(u, hist, pos0, pool_w, pool_b, pool_scale):
    b, L, _ = u.shape
    full = jnp.concatenate([hist, u], axis=1)
    cs = jnp.cumsum(full.astype(jnp.float32), axis=1)
    cs = jnp.concatenate([jnp.zeros((b, 1, POOL_WIDTH), jnp.float32), cs], axis=1)
    end = cs[:, POOL_HIST + 1:]
    pos = (jnp.arange(L) + pos0).astype(jnp.float32)
    means = []
    for g, w in enumerate(POOL_WINDOWS):
        lo, hi = g * POOL_GROUP_DIM, (g + 1) * POOL_GROUP_DIM
        start = cs[:, POOL_HIST + 1 - w:POOL_HIST + 1 - w + L, lo:hi]
        cnt = jnp.minimum(jnp.float32(w), pos + 1.0)[None, :, None]
        means.append((end[:, :, lo:hi] - start) / cnt)
    pooled = jnp.stack(means, axis=2) - u.reshape(b, L, POOL_GROUPS, POOL_GROUP_DIM).astype(jnp.float32)
    out = jnp.einsum('blgc,gcd->blgd', pooled.astype(u.dtype), pool_w) + pool_b
    return out.reshape(b, L, POOL_WIDTH) * pool_scale, full[:, -POOL_HIST:]


def causal_conv(xbc, hist, conv_w, conv_b):
    L = xbc.shape[1]
    full = jnp.concatenate([hist, xbc], axis=1)
    out = conv_b
    for k in range(SSD_CONV):
        out = out + conv_w[k] * full[:, k:k + L]
    return jax.nn.silu(out), full[:, -(SSD_CONV - 1):]


def ssd_scan(xh, dt, A, Bm, Cm, h0):
    b, L, H, P = xh.shape
    G, Hg, N = SSD_GROUPS, SSD_HEADS_PER_GROUP, SSD_STATE
    Q = min(SSD_CHUNK, L)
    nc = L // Q
    x = xh.astype(jnp.float32).reshape(b, nc, Q, G, Hg, P)
    dtc = dt.reshape(b, nc, Q, G, Hg)
    Bc = Bm.astype(jnp.float32).reshape(b, nc, Q, G, N)
    Cc = Cm.astype(jnp.float32).reshape(b, nc, Q, G, N)
    a = dtc * A.reshape(G, Hg)
    acs = jnp.moveaxis(jnp.cumsum(a, axis=2), 2, -1)
    xdt = x * dtc[..., None]
    mask = jnp.tril(jnp.ones((Q, Q), dtype=bool))
    seg = acs[..., :, None] - acs[..., None, :]
    Lmat = jnp.where(mask, jnp.exp(jnp.where(mask, seg, 0.0)), 0.0)
    CB = jnp.einsum('bcign,bcjgn->bcgij', Cc, Bc)
    M = CB[:, :, :, None] * Lmat
    y_diag = jnp.einsum('bcghij,bcjghp->bcighp', M, xdt)
    decay_states = jnp.exp(acs[..., -1:] - acs)
    states = jnp.einsum('bcjgn,bcghj,bcjghp->bcghpn', Bc, decay_states, xdt)
    chunk_decay = jnp.exp(acs[..., -1])

    def step(h, inp):
        dec, st = inp
        return h * dec[..., None, None] + st, h

    h_init = h0.astype(jnp.float32).reshape(b, G, Hg, P, N)
    h_final, h_enter = lax.scan(step, h_init, (jnp.moveaxis(chunk_decay, 1, 0), jnp.moveaxis(states, 1, 0)))
    h_enter = jnp.moveaxis(h_enter, 0, 1)
    y_off = jnp.einsum('bcign,bcghpn,bcghi->bcighp', Cc, h_enter, jnp.exp(acs))
    y = (y_diag + y_off).reshape(b, L, H, P)
    return y, h_final.reshape(b, H, P, N)


def ssd_mixer(z, xbc, dt_raw, conv_hist, h0, conv_w, conv_b, dt_bias, a_log, d_skip, ssd_norm_w):
    b, L, _ = z.shape
    xbc, new_conv = causal_conv(xbc, conv_hist, conv_w, conv_b)
    xs, Bm, Cm = jnp.split(xbc, [SSD_WIDTH, SSD_WIDTH + SSD_GROUPS * SSD_STATE], axis=-1)
    xh = xs.reshape(b, L, SSD_HEADS, SSD_HEAD_DIM)
    Bm = Bm.reshape(b, L, SSD_GROUPS, SSD_STATE)
    Cm = Cm.reshape(b, L, SSD_GROUPS, SSD_STATE)
    dt = jax.nn.softplus(dt_raw.astype(jnp.float32) + dt_bias.astype(jnp.float32))
    A = -jnp.exp(a_log.astype(jnp.float32))
    y, h_new = ssd_scan(xh, dt, A, Bm, Cm, h0)
    y = y + d_skip.astype(jnp.float32)[:, None] * xh.astype(jnp.float32)
    y = y.reshape(b, L, SSD_WIDTH) * jax.nn.silu(z.astype(jnp.float32))
    yg = y.reshape(b, L, SSD_GROUPS, SSD_WIDTH // SSD_GROUPS)
    yg = yg * lax.rsqrt(jnp.mean(yg * yg, axis=-1, keepdims=True) + EPS)
    y = yg.reshape(b, L, SSD_WIDTH).astype(z.dtype) * ssd_norm_w
    return y, new_conv, h_new.astype(h0.dtype)


def trunk_layer(x, c, pool_hist, conv_hist, h0, pos0, norm_mix_w, norm_ffn_w, w_ada, b_ada, w_in,
                pool_w, pool_b, pool_scale, conv_w, conv_b, dt_bias, a_log, d_skip, ssd_norm_w,
                w_out, w_ff1, b_ff1, w_ff2, b_ff2):
    mod = (jax.nn.silu(c) @ w_ada + b_ada)[:, None, :]
    sh1, sc1, g1, sh2, sc2, g2 = jnp.split(mod, 6, axis=-1)
    h = rms_norm(x, norm_mix_w) * (1.0 + sc1) + sh1
    proj = h @ w_in
    u, z, xbc, dt_raw = jnp.split(proj, [POOL_WIDTH, POOL_WIDTH + SSD_WIDTH,
                                         POOL_WIDTH + SSD_WIDTH + CONV_DIM], axis=-1)
    ya, new_pool = pool_mixer(u, pool_hist, pos0, pool_w, pool_b, pool_scale)
    yb, new_conv, new_h = ssd_mixer(z, xbc, dt_raw, conv_hist, h0, conv_w, conv_b,
                                    dt_bias, a_log, d_skip, ssd_norm_w)
    x = x + g1 * (jnp.concatenate([ya, yb], axis=-1) @ w_out)
    h = rms_norm(x, norm_ffn_w) * (1.0 + sc2) + sh2
    f = jnp.square(jax.nn.relu(h @ w_ff1 + b_ff1)) @ w_ff2 + b_ff2
    x = x + g2 * f
    return x, new_pool, new_conv, new_h


def setup_inputs(seed: int = 0) -> dict:
    key = jax.random.key(seed)
    ks = jax.random.split(key, 32)
    f32 = jnp.float32
    nrm = lambda k, shape, s=1.0: jax.random.normal(k, shape, f32) * s
    dt0 = jnp.exp(jax.random.uniform(ks[17], (DEPTH, SSD_HEADS), f32) * (np.log(0.1) - np.log(0.001)) + np.log(0.001))
    dt_bias = dt0 + jnp.log(-jnp.expm1(-dt0))
    return {
        "x_prompt": nrm(ks[0], (BATCH, SEQ, D_MODEL)),
        "x_sample": nrm(ks[1], (DEC_BATCH, DEC_SEQ, D_MODEL)),
        "cache_pool": nrm(ks[2], (DEPTH, DEC_BATCH, POOL_HIST, POOL_WIDTH)),
        "cache_conv": nrm(ks[3], (DEPTH, DEC_BATCH, SSD_CONV - 1, CONV_DIM)),
        "state_ssm": nrm(ks[4], (DEPTH, DEC_BATCH, SSD_HEADS, SSD_HEAD_DIM, SSD_STATE), 0.5),
        "c_prompt": nrm(ks[5], (BATCH, D_MODEL)),
        "c_sample": nrm(ks[6], (DEC_BATCH, D_MODEL)),
        "norm_mix_w": 1.0 + nrm(ks[7], (DEPTH, D_MODEL), 0.05),
        "norm_ffn_w": 1.0 + nrm(ks[8], (DEPTH, D_MODEL), 0.05),
        "w_ada": nrm(ks[9], (DEPTH, D_MODEL, 6 * D_MODEL), 0.5 * D_MODEL ** -0.5),
        "b_ada": nrm(ks[10], (DEPTH, 6 * D_MODEL), 0.02),
        "w_in": nrm(ks[11], (DEPTH, D_MODEL, D_IN_PROJ), D_MODEL ** -0.5),
        "pool_w": nrm(ks[12], (DEPTH, POOL_GROUPS, POOL_GROUP_DIM, POOL_GROUP_DIM), POOL_GROUP_DIM ** -0.5),
        "pool_b": nrm(ks[13], (DEPTH, POOL_GROUPS, POOL_GROUP_DIM), 0.02),
        "pool_scale": 1.0 + nrm(ks[14], (DEPTH, POOL_WIDTH), 0.1),
        "conv_w": nrm(ks[15], (DEPTH, SSD_CONV, CONV_DIM), 0.5),
        "conv_b": nrm(ks[16], (DEPTH, CONV_DIM), 0.02),
        "dt_bias": dt_bias,
        "a_log": jnp.log(jax.random.uniform(ks[18], (DEPTH, SSD_HEADS), f32, 1.0, 16.0)),
        "d_skip": 1.0 + nrm(ks[19], (DEPTH, SSD_HEADS), 0.1),
        "ssd_norm_w": 1.0 + nrm(ks[20], (DEPTH, SSD_WIDTH), 0.05),
        "w_out": nrm(ks[21], (DEPTH, MIX_WIDTH, D_MODEL), MIX_WIDTH ** -0.5),
        "w_ff1": nrm(ks[22], (DEPTH, D_MODEL, D_FF), D_MODEL ** -0.5),
        "b_ff1": nrm(ks[23], (DEPTH, D_FF), 0.02),
        "w_ff2": nrm(ks[24], (DEPTH, D_FF, D_MODEL), D_FF ** -0.5),
        "b_ff2": nrm(ks[25], (DEPTH, D_MODEL), 0.02),
        "final_norm_w": 1.0 + nrm(ks[26], (D_MODEL,), 0.05),
    }


def reference(x_prompt, x_sample, cache_pool, cache_conv, state_ssm, c_prompt, c_sample,
              norm_mix_w, norm_ffn_w, w_ada, b_ada, w_in, pool_w, pool_b, pool_scale,
              conv_w, conv_b, dt_bias, a_log, d_skip, ssd_norm_w, w_out,
              w_ff1, b_ff1, w_ff2, b_ff2, final_norm_w):
    dt = x_prompt.dtype
    xp, xs = x_prompt, x_sample
    pool_p, conv_p, ssm_p = [], [], []
    pool_s, conv_s, ssm_s = [], [], []
    for l in range(DEPTH):
        w = (norm_mix_w[l], norm_ffn_w[l], w_ada[l], b_ada[l], w_in[l], pool_w[l], pool_b[l],
             pool_scale[l], conv_w[l], conv_b[l], dt_bias[l], a_log[l], d_skip[l], ssd_norm_w[l],
             w_out[l], w_ff1[l], b_ff1[l], w_ff2[l], b_ff2[l])
        zp = jnp.zeros((BATCH, POOL_HIST, POOL_WIDTH), dt)
        zc = jnp.zeros((BATCH, SSD_CONV - 1, CONV_DIM), dt)
        zh = jnp.zeros((BATCH, SSD_HEADS, SSD_HEAD_DIM, SSD_STATE), dt)
        xp, npool, nconv, nh = trunk_layer(xp, c_prompt, zp, zc, zh, 0, *w)
        pool_p.append(npool); conv_p.append(nconv); ssm_p.append(nh)
        xs, npool, nconv, nh = trunk_layer(xs, c_sample, cache_pool[l], cache_conv[l], state_ssm[l], PAST_LEN, *w)
        pool_s.append(npool); conv_s.append(nconv); ssm_s.append(nh)
    y_prompt = rms_norm(xp, final_norm_w)
    y_sample = rms_norm(xs, final_norm_w)
    return (y_prompt, y_sample, jnp.stack(pool_p, 0), jnp.stack(conv_p, 0), jnp.stack(ssm_p, 0),
            jnp.stack(pool_s, 0), jnp.stack(conv_s, 0), jnp.stack(ssm_s, 0))
```

```python
import functools

import jax
import jax.numpy as jnp
from jax import lax
from jax.experimental import pallas as pl
from jax.experimental.pallas import tpu as pltpu

D_MODEL = 1024
POOL_WIDTH = 1024
POOL_GROUPS = 4
POOL_GROUP_DIM = POOL_WIDTH // POOL_GROUPS
POOL_WINDOWS = (2, 4, 8, 16)
POOL_HIST = 15
POOL_HIST_ROWS = 16
SSD_WIDTH = 1024
SSD_HEAD_DIM = 64
SSD_HEADS = 16
SSD_GROUPS = 2
SSD_GROUP_WIDTH = SSD_WIDTH // SSD_GROUPS
SSD_STATE = 128
SSD_CONV = 4
CONV_HIST = SSD_CONV - 1
CONV_HIST_ROWS = 8
CONV_DIM = SSD_WIDTH + 2 * SSD_GROUPS * SSD_STATE
D_FF = 4 * D_MODEL
EPS = 1e-6

V7X_VMEM_LIMIT_BYTES = 56 * 1024 * 1024

_BF16 = jnp.bfloat16
_F32 = jnp.float32


def _dot(a, b):
    return jnp.dot(a, b, preferred_element_type=_F32)


def _dot_nt(a, b):
    return lax.dot_general(a, b, (((1,), (1,)), ((), ())), preferred_element_type=_F32)


def _dot_tn(a, b):
    return lax.dot_general(a, b, (((0,), (0,)), ((), ())), preferred_element_type=_F32)


def _split3(v):
    v1 = v.astype(_BF16)
    r1 = v - v1.astype(_F32)
    v2 = r1.astype(_BF16)
    v3 = (r1 - v2.astype(_F32)).astype(_BF16)
    return v1, v2, v3


def _split2(v):
    v1 = v.astype(_BF16)
    v2 = (v - v1.astype(_F32)).astype(_BF16)
    return v1, v2


def _silu(v):
    return v / (1.0 + jnp.exp(-v))


def _softplus(v):
    return jnp.maximum(v, 0.0) + jnp.log1p(jnp.exp(-jnp.abs(v)))


def _rms(v):
    return v * lax.rsqrt(jnp.mean(v * v, axis=-1, keepdims=True) + EPS)


def _ada_kernel(c_ref, w_ref, b_ref, o_ref):
    s = _silu(c_ref[...]).astype(_BF16)
    o_ref[...] = _dot(s, w_ref[...].astype(_BF16)) + b_ref[...]


def _ada(c_all, w_ada, b_ada):
    nb = c_all.shape[0]
    n_out = w_ada.shape[1]
    blk = D_MODEL
    return pl.pallas_call(
        _ada_kernel,
        grid=(n_out // blk,),
        in_specs=[
            pl.BlockSpec((nb, D_MODEL), lambda j: (0, 0)),
            pl.BlockSpec((D_MODEL, blk), lambda j: (0, j)),
            pl.BlockSpec((1, blk), lambda j: (0, j)),
        ],
        out_specs=pl.BlockSpec((nb, blk), lambda j: (0, j)),
        out_shape=jax.ShapeDtypeStruct((nb, n_out), _F32),
        compiler_params=pltpu.CompilerParams(dimension_semantics=("arbitrary",)),
        name="ada",
    )(c_all, w_ada, b_ada.reshape(1, n_out))


def _mixer_kernel(x_ref, mod_ref, ph_ref, ch_ref, h0_ref,
                  nw_ref, wu_ref, wz_ref, wxbc_ref, wdt_ref, wdtt_ref,
                  pw_ref, pb_ref, ps_ref, cw_ref, cb_ref,
                  dtb_r_ref, dtb_c_ref, alog_r_ref, alog_c_ref, dskip_ref, snw_ref, wout_ref,
                  x1_ref, npool_ref, nconv_ref, nssm_ref,
                  ubuf, cbuf, st_ref, *, tl, pos0, n_steps):
    s = pl.program_id(1)
    q = tl

    @pl.when(s == 0)
    def _load_history():
        ubuf[0:POOL_HIST_ROWS, :] = ph_ref[0]
        cbuf[0:CONV_HIST_ROWS, :] = ch_ref[0]
        st_ref[...] = h0_ref[0].reshape(SSD_WIDTH, SSD_STATE).T

    x = x_ref[0]
    mod = mod_ref[0]
    sh1, sc1, g1 = mod[0:1], mod[1:2], mod[2:3]
    h = (_rms(x) * nw_ref[...]) * (1.0 + sc1) + sh1
    hb = h.astype(_BF16)

    u = _dot(hb, wu_ref[...])
    z = _dot(hb, wz_ref[...])
    xbc = _dot(hb, wxbc_ref[...])
    dtr_c = _dot(hb, wdt_ref[...])
    dtr_r = _dot_nt(wdtt_ref[...], hb)

    ubuf[POOL_HIST_ROWS:POOL_HIST_ROWS + q, :] = u
    pos = (lax.broadcasted_iota(jnp.int32, (q, POOL_GROUP_DIM), 0) + (pos0 + s * tl)).astype(_F32)
    ya_parts = []
    for g, w in enumerate(POOL_WINDOWS):
        lo = g * POOL_GROUP_DIM
        acc = ubuf[POOL_HIST_ROWS:POOL_HIST_ROWS + q, lo:lo + POOL_GROUP_DIM]
        for k in range(1, w):
            acc = acc + ubuf[POOL_HIST_ROWS - k:POOL_HIST_ROWS - k + q, lo:lo + POOL_GROUP_DIM]
        cnt = jnp.minimum(jnp.float32(w), pos + 1.0)
        pooled = acc / cnt - u[:, lo:lo + POOL_GROUP_DIM]
        ya_parts.append(_dot(pooled.astype(_BF16), pw_ref[g]))
    ya = (jnp.concatenate(ya_parts, axis=1) + pb_ref[...]) * ps_ref[...]

    cbuf[CONV_HIST_ROWS:CONV_HIST_ROWS + q, :] = xbc
    conv = cb_ref[...]
    for k in range(SSD_CONV):
        r0 = CONV_HIST_ROWS - CONV_HIST + k
        conv = conv + cw_ref[k:k + 1, :] * cbuf[r0:r0 + q, :]
    xbc_act = _silu(conv)
    xs = xbc_act[:, :SSD_WIDTH]
    b_off = SSD_WIDTH
    c_off = SSD_WIDTH + SSD_GROUPS * SSD_STATE

    dt_c = _softplus(dtr_c + dtb_r_ref[...])
    dt_r = _softplus(dtr_r + dtb_c_ref[...])
    a_c = dt_c * (-jnp.exp(alog_r_ref[...]))
    a_r = dt_r * (-jnp.exp(alog_c_ref[...]))
    ri = lax.broadcasted_iota(jnp.int32, (q, q), 0)
    ci = lax.broadcasted_iota(jnp.int32, (q, q), 1)
    causal = ri >= ci
    tri = jnp.where(causal, 1.0, 0.0).astype(_BF16)
    tri_t = jnp.where(ri <= ci, 1.0, 0.0).astype(_BF16)
    a1, a2, a3 = _split3(a_c)
    acs_c = _dot(tri, a1) + _dot(tri, a2) + _dot(tri, a3)
    b1, b2, b3 = _split3(a_r)
    acs_r = _dot(b1, tri_t) + _dot(b2, tri_t) + _dot(b3, tri_t)

    hi = lax.broadcasted_iota(jnp.int32, (SSD_HEADS, SSD_WIDTH), 0)
    chn = lax.broadcasted_iota(jnp.int32, (SSD_HEADS, SSD_WIDTH), 1)
    expand = jnp.where((chn >= hi * SSD_HEAD_DIM) & (chn < (hi + 1) * SSD_HEAD_DIM), 1.0, 0.0).astype(_BF16)

    def _expand_heads(v):
        v1, v2 = _split2(v)
        return _dot(v1, expand) + _dot(v2, expand)

    eacs = jnp.exp(acs_c)
    w_state = dt_c * jnp.exp(acs_c[q - 1:q, :] - acs_c)
    eacs_rep = _expand_heads(eacs)
    w_state_rep = _expand_heads(w_state)
    chunk_decay_rep = eacs_rep[q - 1:q, :]

    lane = lax.broadcasted_iota(jnp.int32, (q, 2 * SSD_HEAD_DIM), 1)
    first_head = lane < SSD_HEAD_DIM
    yd_parts = []
    cb = []
    for g in range(SSD_GROUPS):
        bg = xbc_act[:, b_off + g * SSD_STATE:b_off + (g + 1) * SSD_STATE].astype(_BF16)
        cg = xbc_act[:, c_off + g * SSD_STATE:c_off + (g + 1) * SSD_STATE].astype(_BF16)
        cb.append(_dot_nt(cg, bg))
    for p in range(SSD_HEADS // 2):
        g = (2 * p) // (SSD_HEADS // SSD_GROUPS)
        ms = []
        for hh in (2 * p, 2 * p + 1):
            seg = acs_c[:, hh:hh + 1] - acs_r[hh:hh + 1, :]
            lmat = jnp.where(causal, jnp.exp(jnp.where(causal, seg, 0.0)), 0.0)
            ms.append((cb[g] * lmat * dt_r[hh:hh + 1, :]).astype(_BF16))
        xp = xs[:, p * 2 * SSD_HEAD_DIM:(p + 1) * 2 * SSD_HEAD_DIM]
        rhs = jnp.concatenate([jnp.where(first_head, xp, 0.0), jnp.where(first_head, 0.0, xp)],
                              axis=0).astype(_BF16)
        yd_parts.append(_dot(jnp.concatenate(ms, axis=1), rhs))
    y_diag = jnp.concatenate(yd_parts, axis=1)

    xw = (xs * w_state_rep).astype(_BF16)
    yo_parts = []
    for g in range(SSD_GROUPS):
        cols = slice(g * SSD_GROUP_WIDTH, (g + 1) * SSD_GROUP_WIDTH)
        st_g = st_ref[:, cols]
        cg = xbc_act[:, c_off + g * SSD_STATE:c_off + (g + 1) * SSD_STATE].astype(_BF16)
        bg = xbc_act[:, b_off + g * SSD_STATE:b_off + (g + 1) * SSD_STATE].astype(_BF16)
        yo_parts.append(_dot(cg, st_g.astype(_BF16)))
        st_ref[:, cols] = st_g * chunk_decay_rep[:, cols] + _dot_tn(bg, xw[:, cols])
    y_off = jnp.concatenate(yo_parts, axis=1) * eacs_rep

    y = (y_diag + y_off + dskip_ref[...] * xs) * _silu(z)
    yb = jnp.concatenate(
        [_rms(y[:, g * SSD_GROUP_WIDTH:(g + 1) * SSD_GROUP_WIDTH]) for g in range(SSD_GROUPS)],
        axis=1) * snw_ref[...]

    mix = _dot(ya.astype(_BF16), wout_ref[0:POOL_WIDTH, :]) + \
        _dot(yb.astype(_BF16), wout_ref[POOL_WIDTH:POOL_WIDTH + SSD_WIDTH, :])
    x1_ref[0] = x + g1 * mix

    new_pool = ubuf[q:q + POOL_HIST_ROWS, :]
    new_conv = cbuf[q:q + CONV_HIST_ROWS, :]
    ubuf[0:POOL_HIST_ROWS, :] = new_pool
    cbuf[0:CONV_HIST_ROWS, :] = new_conv

    @pl.when(s == n_steps - 1)
    def _emit_caches():
        npool_ref[0] = new_pool
        nconv_ref[0] = new_conv
        nssm_ref[0] = st_ref[...].T.reshape(SSD_HEADS, SSD_HEAD_DIM, SSD_STATE)


def _mixer(x, mod, pool_hist, conv_hist, h0, wts, *, pos0, tl):
    nb, seq, _ = x.shape
    assert seq % tl == 0
    n_steps = seq // tl
    ph = jnp.pad(pool_hist, ((0, 0), (POOL_HIST_ROWS - POOL_HIST, 0), (0, 0)))
    ch = jnp.pad(conv_hist, ((0, 0), (CONV_HIST_ROWS - CONV_HIST, 0), (0, 0)))

    def const(shape):
        return pl.BlockSpec(shape, lambda b, s: (0,) * len(shape))

    in_specs = [
        pl.BlockSpec((1, tl, D_MODEL), lambda b, s: (b, s, 0)),
        pl.BlockSpec((1, 6, D_MODEL), lambda b, s: (b, 0, 0)),
        pl.BlockSpec((1, POOL_HIST_ROWS, POOL_WIDTH), lambda b, s: (b, 0, 0)),
        pl.BlockSpec((1, CONV_HIST_ROWS, CONV_DIM), lambda b, s: (b, 0, 0)),
        pl.BlockSpec((1, SSD_HEADS, SSD_HEAD_DIM, SSD_STATE), lambda b, s: (b, 0, 0, 0)),
        const((1, D_MODEL)),
        const((D_MODEL, POOL_WIDTH)),
        const((D_MODEL, SSD_WIDTH)),
        const((D_MODEL, CONV_DIM)),
        const((D_MODEL, SSD_HEADS)),
        const((SSD_HEADS, D_MODEL)),
        const((POOL_GROUPS, POOL_GROUP_DIM, POOL_GROUP_DIM)),
        const((1, POOL_WIDTH)),
        const((1, POOL_WIDTH)),
        const((SSD_CONV, CONV_DIM)),
        const((1, CONV_DIM)),
        const((1, SSD_HEADS)),
        const((SSD_HEADS, 1)),
        const((1, SSD_HEADS)),
        const((SSD_HEADS, 1)),
        const((1, SSD_WIDTH)),
        const((1, SSD_WIDTH)),
        const((POOL_WIDTH + SSD_WIDTH, D_MODEL)),
    ]
    out_specs = [
        pl.BlockSpec((1, tl, D_MODEL), lambda b, s: (b, s, 0)),
        pl.BlockSpec((1, POOL_HIST_ROWS, POOL_WIDTH), lambda b, s: (b, 0, 0)),
        pl.BlockSpec((1, CONV_HIST_ROWS, CONV_DIM), lambda b, s: (b, 0, 0)),
        pl.BlockSpec((1, SSD_HEADS, SSD_HEAD_DIM, SSD_STATE), lambda b, s: (b, 0, 0, 0)),
    ]
    out_shape = [
        jax.ShapeDtypeStruct((nb, seq, D_MODEL), _F32),
        jax.ShapeDtypeStruct((nb, POOL_HIST_ROWS, POOL_WIDTH), _F32),
        jax.ShapeDtypeStruct((nb, CONV_HIST_ROWS, CONV_DIM), _F32),
        jax.ShapeDtypeStruct((nb, SSD_HEADS, SSD_HEAD_DIM, SSD_STATE), _F32),
    ]
    x1, npool, nconv, nssm = pl.pallas_call(
        functools.partial(_mixer_kernel, tl=tl, pos0=pos0, n_steps=n_steps),
        grid=(nb, n_steps),
        in_specs=in_specs,
        out_specs=out_specs,
        out_shape=out_shape,
        scratch_shapes=[
            pltpu.VMEM((POOL_HIST_ROWS + tl, POOL_WIDTH), _F32),
            pltpu.VMEM((CONV_HIST_ROWS + tl, CONV_DIM), _F32),
            pltpu.VMEM((SSD_STATE, SSD_WIDTH), _F32),
        ],
        compiler_params=pltpu.CompilerParams(
            dimension_semantics=("arbitrary", "arbitrary"),
            vmem_limit_bytes=V7X_VMEM_LIMIT_BYTES),
        name="mixer",
    )(x, mod, ph, ch, h0, *wts)
    return x1, npool[:, POOL_HIST_ROWS - POOL_HIST:], nconv[:, CONV_HIST_ROWS - CONV_HIST:], nssm


def _ffn_kernel(x_ref, mod_ref, nw_ref, w1_ref, b1_ref, w2_ref, b2_ref, fw_ref, y_ref, *, ff_chunk):
    tb, tl, _ = x_ref.shape
    x3 = x_ref[...]
    mod = mod_ref[...]
    sh2, sc2, g2 = mod[:, 3:4, :], mod[:, 4:5, :], mod[:, 5:6, :]
    h3 = (_rms(x3) * nw_ref[...]) * (1.0 + sc2) + sh2
    hb = h3.reshape(tb * tl, D_MODEL).astype(_BF16)
    acc = None
    for j in range(D_FF // ff_chunk):
        cols = slice(j * ff_chunk, (j + 1) * ff_chunk)
        f = jnp.maximum(_dot(hb, w1_ref[:, cols]) + b1_ref[:, cols], 0.0)
        part = _dot((f * f).astype(_BF16), w2_ref[cols, :])
        acc = part if acc is None else acc + part
    f2 = (acc + b2_ref[...]).reshape(tb, tl, D_MODEL)
    x2 = x3 + g2 * f2
    y_ref[...] = _rms(x2) * fw_ref[...]


def _ffn(x1, mod, wts, *, tb, tl):
    nb, seq, _ = x1.shape
    assert nb % tb == 0 and seq % tl == 0

    def const(shape):
        return pl.BlockSpec(shape, lambda b, s: (0,) * len(shape))

    return pl.pallas_call(
        functools.partial(_ffn_kernel, ff_chunk=1024),
        grid=(nb // tb, seq // tl),
        in_specs=[
            pl.BlockSpec((tb, tl, D_MODEL), lambda b, s: (b, s, 0)),
            pl.BlockSpec((tb, 6, D_MODEL), lambda b, s: (b, 0, 0)),
            const((1, D_MODEL)),
            const((D_MODEL, D_FF)),
            const((1, D_FF)),
            const((D_FF, D_MODEL)),
            const((1, D_MODEL)),
            const((1, D_MODEL)),
        ],
        out_specs=pl.BlockSpec((tb, tl, D_MODEL), lambda b, s: (b, s, 0)),
        out_shape=jax.ShapeDtypeStruct((nb, seq, D_MODEL), _F32),
        compiler_params=pltpu.CompilerParams(
            dimension_semantics=("arbitrary", "arbitrary"),
            vmem_limit_bytes=V7X_VMEM_LIMIT_BYTES),
        name="ffn",
    )(x1, mod, *wts)


def _layer_weights(l, norm_mix_w, norm_ffn_w, w_in, pool_w, pool_b, pool_scale, conv_w, conv_b,
                   dt_bias, a_log, d_skip, ssd_norm_w, w_out, w_ff1, b_ff1, w_ff2, b_ff2):
    win = w_in[l]
    o_z = POOL_WIDTH
    o_xbc = POOL_WIDTH + SSD_WIDTH
    o_dt = o_xbc + CONV_DIM
    w_dt = win[:, o_dt:o_dt + SSD_HEADS]
    mixer_w = (
        norm_mix_w[l].reshape(1, D_MODEL),
        win[:, :o_z].astype(_BF16),
        win[:, o_z:o_xbc].astype(_BF16),
        win[:, o_xbc:o_dt].astype(_BF16),
        w_dt.astype(_BF16),
        w_dt.T.astype(_BF16),
        pool_w[l].astype(_BF16),
        pool_b[l].reshape(1, POOL_WIDTH),
        pool_scale[l].reshape(1, POOL_WIDTH),
        conv_w[l],
        conv_b[l].reshape(1, CONV_DIM),
        dt_bias[l].reshape(1, SSD_HEADS),
        dt_bias[l].reshape(SSD_HEADS, 1),
        a_log[l].reshape(1, SSD_HEADS),
        a_log[l].reshape(SSD_HEADS, 1),
        jnp.repeat(d_skip[l], SSD_HEAD_DIM).reshape(1, SSD_WIDTH),
        ssd_norm_w[l].reshape(1, SSD_WIDTH),
        w_out[l].astype(_BF16),
    )
    ffn_w = (
        norm_ffn_w[l].reshape(1, D_MODEL),
        w_ff1[l].astype(_BF16),
        b_ff1[l].reshape(1, D_FF),
        w_ff2[l].astype(_BF16),
        b_ff2[l].reshape(1, D_MODEL),
    )
    return mixer_w, ffn_w


def _run(x_prompt, x_sample, cache_pool, cache_conv, state_ssm, c_prompt, c_sample,
         norm_mix_w, norm_ffn_w, w_ada, b_ada, w_in, pool_w, pool_b, pool_scale,
         conv_w, conv_b, dt_bias, a_log, d_skip, ssd_norm_w, w_out,
         w_ff1, b_ff1, w_ff2, b_ff2, final_norm_w, *, prompt_tl, ffn_tl, past_len):
    depth = w_in.shape[0]
    assert depth == 1, "the final norm is fused into the (single) layer's ffn call"
    nbp, seq_p, _ = x_prompt.shape
    nbs, seq_s, _ = x_sample.shape
    fw = final_norm_w.reshape(1, D_MODEL)
    c_all = jnp.concatenate([c_prompt, c_sample], axis=0)
    xp, xs = x_prompt, x_sample
    outs = {k: [] for k in ("pp", "cp", "sp", "ps", "cs", "ss")}
    for l in range(depth):
        mod = _ada(c_all, w_ada[l], b_ada[l]).reshape(nbp + nbs, 6, D_MODEL)
        mod_p, mod_s = mod[:nbp], mod[nbp:]
        mixer_w, ffn_w = _layer_weights(l, norm_mix_w, norm_ffn_w, w_in, pool_w, pool_b, pool_scale,
                                        conv_w, conv_b, dt_bias, a_log, d_skip, ssd_norm_w, w_out,
                                        w_ff1, b_ff1, w_ff2, b_ff2)
        zp = jnp.zeros((nbp, POOL_HIST, POOL_WIDTH), _F32)
        zc = jnp.zeros((nbp, CONV_HIST, CONV_DIM), _F32)
        zh = jnp.zeros((nbp, SSD_HEADS, SSD_HEAD_DIM, SSD_STATE), _F32)
        xp, npool, nconv, nssm = _mixer(xp, mod_p, zp, zc, zh, mixer_w, pos0=0, tl=prompt_tl)
        xp = _ffn(xp, mod_p, ffn_w + (fw,), tb=1, tl=ffn_tl)
        outs["pp"].append(npool); outs["cp"].append(nconv); outs["sp"].append(nssm)
        xs, npool, nconv, nssm = _mixer(xs, mod_s, cache_pool[l], cache_conv[l], state_ssm[l], mixer_w,
                                        pos0=past_len, tl=seq_s)
        xs = _ffn(xs, mod_s, ffn_w + (fw,), tb=nbs, tl=seq_s)
        outs["ps"].append(npool); outs["cs"].append(nconv); outs["ss"].append(nssm)
    st = lambda k: jnp.stack(outs[k], 0)
    return (xp, xs, st("pp"), st("cp"), st("sp"), st("ps"), st("cs"), st("ss"))


def kernel(x_prompt, x_sample, cache_pool, cache_conv, state_ssm, c_prompt, c_sample, norm_mix_w, norm_ffn_w, w_ada, b_ada, w_in, pool_w, pool_b, pool_scale, conv_w, conv_b, dt_bias, a_log, d_skip, ssd_norm_w, w_out, w_ff1, b_ff1, w_ff2, b_ff2, final_norm_w):
    return _run(x_prompt, x_sample, cache_pool, cache_conv, state_ssm, c_prompt, c_sample,
                norm_mix_w, norm_ffn_w, w_ada, b_ada, w_in, pool_w, pool_b, pool_scale,
                conv_w, conv_b, dt_bias, a_log, d_skip, ssd_norm_w, w_out,
                w_ff1, b_ff1, w_ff2, b_ff2, final_norm_w,
                prompt_tl=128, ffn_tl=512, past_len=4096)
```

```python
import functools

import jax
import jax.numpy as jnp
from jax import lax
from jax.experimental import pallas as pl
from jax.experimental.pallas import tpu as pltpu

D_MODEL = 1024
POOL_WIDTH = 1024
POOL_GROUPS = 4
POOL_GROUP_DIM = POOL_WIDTH // POOL_GROUPS
POOL_WINDOWS = (2, 4, 8, 16)
POOL_HIST = 15
POOL_HIST_ROWS = 16
SSD_WIDTH = 1024
SSD_HEAD_DIM = 64
SSD_HEADS = 16
SSD_GROUPS = 2
SSD_GROUP_WIDTH = SSD_WIDTH // SSD_GROUPS
SSD_STATE = 128
SSD_CONV = 4
CONV_HIST = SSD_CONV - 1
CONV_HIST_ROWS = 8
CONV_DIM = SSD_WIDTH + 2 * SSD_GROUPS * SSD_STATE
D_FF = 4 * D_MODEL
EPS = 1e-6

V7X_VMEM_LIMIT_BYTES = 56 * 1024 * 1024
LANES = 128
SSD_CHUNK = 128

_BF16 = jnp.bfloat16
_F32 = jnp.float32


def _dot(a, b):
    return jnp.dot(a, b, preferred_element_type=_F32)


def _dot_nt(a, b):
    return lax.dot_general(a, b, (((1,), (1,)), ((), ())), preferred_element_type=_F32)


def _dot_tn(a, b):
    return lax.dot_general(a, b, (((0,), (0,)), ((), ())), preferred_element_type=_F32)


def _split3(v):
    v1 = v.astype(_BF16)
    r1 = v - v1.astype(_F32)
    v2 = r1.astype(_BF16)
    v3 = (r1 - v2.astype(_F32)).astype(_BF16)
    return v1, v2, v3


def _split2(v):
    v1 = v.astype(_BF16)
    v2 = (v - v1.astype(_F32)).astype(_BF16)
    return v1, v2


def _silu(v):
    hv = 0.5 * v
    return hv + hv * jnp.tanh(hv)


def _softplus(v):
    return jnp.maximum(v, 0.0) + jnp.log1p(jnp.exp(-jnp.abs(v)))


def _rms(v):
    return v * lax.rsqrt(jnp.mean(v * v, axis=-1, keepdims=True) + EPS)


def _shift_rows(v, k):
    return pltpu.roll(v, k, axis=0)


def _ada_kernel(c_ref, w_ref, b_ref, o_ref):
    s = _silu(c_ref[...]).astype(_BF16)
    o_ref[...] = _dot(s, w_ref[...].astype(_BF16)) + b_ref[...]


def _ada(c_all, w_ada, b_ada):
    nb = c_all.shape[0]
    n_out = w_ada.shape[1]
    blk = D_MODEL
    return pl.pallas_call(
        _ada_kernel,
        grid=(n_out // blk,),
        in_specs=[
            pl.BlockSpec((nb, D_MODEL), lambda j: (0, 0)),
            pl.BlockSpec((D_MODEL, blk), lambda j: (0, j)),
            pl.BlockSpec((1, blk), lambda j: (0, j)),
        ],
        out_specs=pl.BlockSpec((nb, blk), lambda j: (0, j)),
        out_shape=jax.ShapeDtypeStruct((nb, n_out), _F32),
        compiler_params=pltpu.CompilerParams(dimension_semantics=("arbitrary",)),
        name="ada",
    )(c_all, w_ada, b_ada.reshape(1, n_out))


def _ssd_chunk(xs, bm, cm, dtr, st_ref, dtb, neg_a, dskip):
    q = xs.shape[0]
    dt_c = _softplus(dtr + dtb)
    a_c = dt_c * neg_a
    ri = lax.broadcasted_iota(jnp.int32, (q, q), 0)
    ci = lax.broadcasted_iota(jnp.int32, (q, q), 1)
    causal = ri >= ci
    tri = jnp.where(causal, 1.0, 0.0).astype(_BF16)
    a1, a2, a3 = _split3(a_c)
    acs_c = _dot(jnp.concatenate([tri, tri, tri], axis=1),
                 jnp.concatenate([a1, a2, a3], axis=0))

    side = max(q, LANES)
    cols = jnp.concatenate([dt_c, acs_c, jnp.zeros((q, LANES - 2 * SSD_HEADS), _F32)], axis=1)
    if q < side:
        cols = jnp.concatenate([cols, jnp.zeros((side - q, LANES), _F32)], axis=0)
    rows = cols.T
    dt_r = rows[0:SSD_HEADS, 0:q]
    acs_r = rows[SSD_HEADS:2 * SSD_HEADS, 0:q]

    hi = lax.broadcasted_iota(jnp.int32, (2 * SSD_HEADS, SSD_WIDTH), 0) % SSD_HEADS
    chn = lax.broadcasted_iota(jnp.int32, (2 * SSD_HEADS, SSD_WIDTH), 1)
    expand2 = jnp.where((chn >= hi * SSD_HEAD_DIM) & (chn < (hi + 1) * SSD_HEAD_DIM),
                        1.0, 0.0).astype(_BF16)

    def _expand_heads(v):
        return _dot(jnp.concatenate(_split2(v), axis=1), expand2)

    eacs_rep = _expand_heads(jnp.exp(acs_c))
    w_state_rep = _expand_heads(dt_c * jnp.exp(acs_c[q - 1:q, :] - acs_c))
    chunk_decay_rep = eacs_rep[q - 1:q, :]

    lane = lax.broadcasted_iota(jnp.int32, (q, 2 * SSD_HEAD_DIM), 1)
    first_head = lane < SSD_HEAD_DIM
    bgs = [bm[:, g * SSD_STATE:(g + 1) * SSD_STATE].astype(_BF16) for g in range(SSD_GROUPS)]
    cgs = [cm[:, g * SSD_STATE:(g + 1) * SSD_STATE].astype(_BF16) for g in range(SSD_GROUPS)]
    cbm = [jnp.where(causal, _dot_nt(cgs[g], bgs[g]), 0.0) for g in range(SSD_GROUPS)]
    yd_parts = []
    for p in range(SSD_HEADS // 2):
        g = (2 * p) // (SSD_HEADS // SSD_GROUPS)
        ms = []
        for hh in (2 * p, 2 * p + 1):
            seg = jnp.minimum(acs_c[:, hh:hh + 1] - acs_r[hh:hh + 1, :], 0.0)
            ms.append((cbm[g] * jnp.exp(seg) * dt_r[hh:hh + 1, :]).astype(_BF16))
        xp = xs[:, p * 2 * SSD_HEAD_DIM:(p + 1) * 2 * SSD_HEAD_DIM]
        rhs = jnp.concatenate([jnp.where(first_head, xp, 0.0), jnp.where(first_head, 0.0, xp)],
                              axis=0).astype(_BF16)
        yd_parts.append(_dot(jnp.concatenate(ms, axis=1), rhs))
    y_diag = jnp.concatenate(yd_parts, axis=1)

    xw = (xs * w_state_rep).astype(_BF16)
    yo_parts = []
    for g in range(SSD_GROUPS):
        gc = slice(g * SSD_GROUP_WIDTH, (g + 1) * SSD_GROUP_WIDTH)
        st_g = st_ref[:, gc]
        yo_parts.append(_dot(cgs[g], st_g.astype(_BF16)))
        st_ref[:, gc] = st_g * chunk_decay_rep[:, gc] + _dot_tn(bgs[g], xw[:, gc])
    y_off = jnp.concatenate(yo_parts, axis=1) * eacs_rep
    return y_diag + y_off + dskip * xs


def _mixer_kernel(x_ref, mod_ref, ph_ref, ch_ref, h0_ref,
                  nw_ref, wu_ref, wz_ref, wxbc_ref, wdt_ref,
                  pw_ref, pb_ref, ps_ref, cw_ref, cb_ref,
                  dtb_ref, alog_ref, dskip_ref, snw_ref, wout_ref,
                  x1_ref, npool_ref, nconv_ref, nssm_ref,
                  ubuf, cbuf, st_ref, *, tl, q, pos0, n_steps):
    s = pl.program_id(1)

    @pl.when(s == 0)
    def _load_history():
        ubuf[0:POOL_HIST_ROWS, :] = ph_ref[0]
        cbuf[0:CONV_HIST_ROWS, :] = ch_ref[0]
        st_ref[...] = h0_ref[0].reshape(SSD_WIDTH, SSD_STATE).T

    x = x_ref[0]
    mod = mod_ref[0]
    sh1, sc1, g1 = mod[0:1], mod[1:2], mod[2:3]
    h = _rms(x) * (nw_ref[...] * (1.0 + sc1)) + sh1
    hb = h.astype(_BF16)

    u = _dot(hb, wu_ref[...])
    z = _dot(hb, wz_ref[...])
    xbc = _dot(hb, wxbc_ref[...])
    dtr = _dot(hb, wdt_ref[...])

    ubuf[POOL_HIST_ROWS:POOL_HIST_ROWS + tl, :] = u
    pos = (lax.broadcasted_iota(jnp.int32, (tl, POOL_GROUP_DIM), 0) + (pos0 + s * tl)).astype(_F32)
    ya_parts = []
    for g, w in enumerate(POOL_WINDOWS):
        lo = g * POOL_GROUP_DIM
        wsum = ubuf[:, lo:lo + POOL_GROUP_DIM]
        k = 1
        while k < w:
            wsum = wsum + _shift_rows(wsum, k)
            k *= 2
        cnt = jnp.minimum(jnp.float32(w), pos + 1.0)
        pooled = wsum[POOL_HIST_ROWS:, :] / cnt - u[:, lo:lo + POOL_GROUP_DIM]
        ya_parts.append(_dot(pooled.astype(_BF16), pw_ref[g]))
    ya = (jnp.concatenate(ya_parts, axis=1) + pb_ref[...]) * ps_ref[...]

    cbuf[CONV_HIST_ROWS:CONV_HIST_ROWS + tl, :] = xbc
    full = cbuf[...]
    prev = _shift_rows(full, 1)
    near = full * cw_ref[3:4, :] + prev * cw_ref[2:3, :]
    far = full * cw_ref[1:2, :] + prev * cw_ref[0:1, :]
    conv = cb_ref[...] + near + _shift_rows(far, 2)
    xbc_act = _silu(conv[CONV_HIST_ROWS:, :])
    b_off = SSD_WIDTH
    c_off = SSD_WIDTH + SSD_GROUPS * SSD_STATE

    neg_a = -jnp.exp(alog_ref[...])
    y_parts = []
    for c in range(tl // q):
        r = slice(c * q, (c + 1) * q)
        y_parts.append(_ssd_chunk(xbc_act[r, 0:SSD_WIDTH], xbc_act[r, b_off:c_off],
                                  xbc_act[r, c_off:CONV_DIM], dtr[r, :], st_ref,
                                  dtb_ref[...], neg_a, dskip_ref[...]))
    y = jnp.concatenate(y_parts, axis=0) if len(y_parts) > 1 else y_parts[0]

    y = y * _silu(z)
    yb = jnp.concatenate(
        [_rms(y[:, g * SSD_GROUP_WIDTH:(g + 1) * SSD_GROUP_WIDTH]) for g in range(SSD_GROUPS)],
        axis=1) * snw_ref[...]

    mix = _dot(ya.astype(_BF16), wout_ref[0:POOL_WIDTH, :]) + \
        _dot(yb.astype(_BF16), wout_ref[POOL_WIDTH:POOL_WIDTH + SSD_WIDTH, :])
    x1_ref[0] = x + g1 * mix

    new_pool = ubuf[tl:tl + POOL_HIST_ROWS, :]
    new_conv = cbuf[tl:tl + CONV_HIST_ROWS, :]
    ubuf[0:POOL_HIST_ROWS, :] = new_pool
    cbuf[0:CONV_HIST_ROWS, :] = new_conv

    @pl.when(s == n_steps - 1)
    def _emit_caches():
        npool_ref[0] = new_pool
        nconv_ref[0] = new_conv
        nssm_ref[0] = st_ref[...].T.reshape(SSD_HEADS, SSD_HEAD_DIM, SSD_STATE)


def _mixer(x, mod, pool_hist, conv_hist, h0, wts, *, pos0, tl):
    nb, seq, _ = x.shape
    assert seq % tl == 0
    n_steps = seq // tl
    q = min(tl, SSD_CHUNK)
    assert tl % q == 0
    ph = jnp.pad(pool_hist, ((0, 0), (POOL_HIST_ROWS - POOL_HIST, 0), (0, 0)))
    ch = jnp.pad(conv_hist, ((0, 0), (CONV_HIST_ROWS - CONV_HIST, 0), (0, 0)))

    def const(shape):
        return pl.BlockSpec(shape, lambda b, s: (0,) * len(shape))

    in_specs = [
        pl.BlockSpec((1, tl, D_MODEL), lambda b, s: (b, s, 0)),
        pl.BlockSpec((1, 6, D_MODEL), lambda b, s: (b, 0, 0)),
        pl.BlockSpec((1, POOL_HIST_ROWS, POOL_WIDTH), lambda b, s: (b, 0, 0)),
        pl.BlockSpec((1, CONV_HIST_ROWS, CONV_DIM), lambda b, s: (b, 0, 0)),
        pl.BlockSpec((1, SSD_HEADS, SSD_HEAD_DIM, SSD_STATE), lambda b, s: (b, 0, 0, 0)),
        const((1, D_MODEL)),
        const((D_MODEL, POOL_WIDTH)),
        const((D_MODEL, SSD_WIDTH)),
        const((D_MODEL, CONV_DIM)),
        const((D_MODEL, SSD_HEADS)),
        const((POOL_GROUPS, POOL_GROUP_DIM, POOL_GROUP_DIM)),
        const((1, POOL_WIDTH)),
        const((1, POOL_WIDTH)),
        const((SSD_CONV, CONV_DIM)),
        const((1, CONV_DIM)),
        const((1, SSD_HEADS)),
        const((1, SSD_HEADS)),
        const((1, SSD_WIDTH)),
        const((1, SSD_WIDTH)),
        const((POOL_WIDTH + SSD_WIDTH, D_MODEL)),
    ]
    out_specs = [
        pl.BlockSpec((1, tl, D_MODEL), lambda b, s: (b, s, 0)),
        pl.BlockSpec((1, POOL_HIST_ROWS, POOL_WIDTH), lambda b, s: (b, 0, 0)),
        pl.BlockSpec((1, CONV_HIST_ROWS, CONV_DIM), lambda b, s: (b, 0, 0)),
        pl.BlockSpec((1, SSD_HEADS, SSD_HEAD_DIM, SSD_STATE), lambda b, s: (b, 0, 0, 0)),
    ]
    out_shape = [
        jax.ShapeDtypeStruct((nb, seq, D_MODEL), _F32),
        jax.ShapeDtypeStruct((nb, POOL_HIST_ROWS, POOL_WIDTH), _F32),
        jax.ShapeDtypeStruct((nb, CONV_HIST_ROWS, CONV_DIM), _F32),
        jax.ShapeDtypeStruct((nb, SSD_HEADS, SSD_HEAD_DIM, SSD_STATE), _F32),
    ]
    x1, npool, nconv, nssm = pl.pallas_call(
        functools.partial(_mixer_kernel, tl=tl, q=q, pos0=pos0, n_steps=n_steps),
        grid=(nb, n_steps),
        in_specs=in_specs,
        out_specs=out_specs,
        out_shape=out_shape,
        scratch_shapes=[
            pltpu.VMEM((POOL_HIST_ROWS + tl, POOL_WIDTH), _F32),
            pltpu.VMEM((CONV_HIST_ROWS + tl, CONV_DIM), _F32),
            pltpu.VMEM((SSD_STATE, SSD_WIDTH), _F32),
        ],
        compiler_params=pltpu.CompilerParams(
            dimension_semantics=("arbitrary", "arbitrary"),
            vmem_limit_bytes=V7X_VMEM_LIMIT_BYTES),
        name="mixer",
    )(x, mod, ph, ch, h0, *wts)
    return x1, npool[:, POOL_HIST_ROWS - POOL_HIST:], nconv[:, CONV_HIST_ROWS - CONV_HIST:], nssm


def _ffn_kernel(x_ref, mod_ref, nw_ref, w1_ref, b1_ref, w2_ref, b2_ref, fw_ref, y_ref, *, ff_chunk):
    tb, tl, _ = x_ref.shape
    x3 = x_ref[...]
    mod = mod_ref[...]
    sh2, sc2, g2 = mod[:, 3:4, :], mod[:, 4:5, :], mod[:, 5:6, :]
    h3 = (_rms(x3) * nw_ref[...]) * (1.0 + sc2) + sh2
    hb = h3.reshape(tb * tl, D_MODEL).astype(_BF16)
    acc = None
    for j in range(D_FF // ff_chunk):
        cols = slice(j * ff_chunk, (j + 1) * ff_chunk)
        f = jnp.maximum(_dot(hb, w1_ref[:, cols]) + b1_ref[:, cols], 0.0)
        part = _dot((f * f).astype(_BF16), w2_ref[cols, :])
        acc = part if acc is None else acc + part
    f2 = (acc + b2_ref[...]).reshape(tb, tl, D_MODEL)
    x2 = x3 + g2 * f2
    y_ref[...] = _rms(x2) * fw_ref[...]


def _ffn(x1, mod, wts, *, tb, tl):
    nb, seq, _ = x1.shape
    assert nb % tb == 0 and seq % tl == 0

    def const(shape):
        return pl.BlockSpec(shape, lambda b, s: (0,) * len(shape))

    return pl.pallas_call(
        functools.partial(_ffn_kernel, ff_chunk=1024),
        grid=(nb // tb, seq // tl),
        in_specs=[
            pl.BlockSpec((tb, tl, D_MODEL), lambda b, s: (b, s, 0)),
            pl.BlockSpec((tb, 6, D_MODEL), lambda b, s: (b, 0, 0)),
            const((1, D_MODEL)),
            const((D_MODEL, D_FF)),
            const((1, D_FF)),
            const((D_FF, D_MODEL)),
            const((1, D_MODEL)),
            const((1, D_MODEL)),
        ],
        out_specs=pl.BlockSpec((tb, tl, D_MODEL), lambda b, s: (b, s, 0)),
        out_shape=jax.ShapeDtypeStruct((nb, seq, D_MODEL), _F32),
        compiler_params=pltpu.CompilerParams(
            dimension_semantics=("arbitrary", "arbitrary"),
            vmem_limit_bytes=V7X_VMEM_LIMIT_BYTES),
        name="ffn",
    )(x1, mod, *wts)


def _layer_weights(l, norm_mix_w, norm_ffn_w, w_in, pool_w, pool_b, pool_scale, conv_w, conv_b,
                   dt_bias, a_log, d_skip, ssd_norm_w, w_out, w_ff1, b_ff1, w_ff2, b_ff2):
    win = w_in[l]
    o_z = POOL_WIDTH
    o_xbc = POOL_WIDTH + SSD_WIDTH
    o_dt = o_xbc + CONV_DIM
    mixer_w = (
        norm_mix_w[l].reshape(1, D_MODEL),
        win[:, :o_z].astype(_BF16),
        win[:, o_z:o_xbc].astype(_BF16),
        win[:, o_xbc:o_dt].astype(_BF16),
        win[:, o_dt:o_dt + SSD_HEADS].astype(_BF16),
        pool_w[l].astype(_BF16),
        pool_b[l].reshape(1, POOL_WIDTH),
        pool_scale[l].reshape(1, POOL_WIDTH),
        conv_w[l],
        conv_b[l].reshape(1, CONV_DIM),
        dt_bias[l].reshape(1, SSD_HEADS),
        a_log[l].reshape(1, SSD_HEADS),
        jnp.repeat(d_skip[l], SSD_HEAD_DIM).reshape(1, SSD_WIDTH),
        ssd_norm_w[l].reshape(1, SSD_WIDTH),
        w_out[l].astype(_BF16),
    )
    ffn_w = (
        norm_ffn_w[l].reshape(1, D_MODEL),
        w_ff1[l].astype(_BF16),
        b_ff1[l].reshape(1, D_FF),
        w_ff2[l].astype(_BF16),
        b_ff2[l].reshape(1, D_MODEL),
    )
    return mixer_w, ffn_w


def _run(x_prompt, x_sample, cache_pool, cache_conv, state_ssm, c_prompt, c_sample,
         norm_mix_w, norm_ffn_w, w_ada, b_ada, w_in, pool_w, pool_b, pool_scale,
         conv_w, conv_b, dt_bias, a_log, d_skip, ssd_norm_w, w_out,
         w_ff1, b_ff1, w_ff2, b_ff2, final_norm_w, *, prompt_tl, ffn_tl, past_len):
    depth = w_in.shape[0]
    assert depth == 1, "the final norm is fused into the (single) layer's ffn call"
    nbp, seq_p, _ = x_prompt.shape
    nbs, seq_s, _ = x_sample.shape
    fw = final_norm_w.reshape(1, D_MODEL)
    c_all = jnp.concatenate([c_prompt, c_sample], axis=0)
    xp, xs = x_prompt, x_sample
    outs = {k: [] for k in ("pp", "cp", "sp", "ps", "cs", "ss")}
    for l in range(depth):
        mod = _ada(c_all, w_ada[l], b_ada[l]).reshape(nbp + nbs, 6, D_MODEL)
        mod_p, mod_s = mod[:nbp], mod[nbp:]
        mixer_w, ffn_w = _layer_weights(l, norm_mix_w, norm_ffn_w, w_in, pool_w, pool_b, pool_scale,
                                        conv_w, conv_b, dt_bias, a_log, d_skip, ssd_norm_w, w_out,
                                        w_ff1, b_ff1, w_ff2, b_ff2)
        zp = jnp.zeros((nbp, POOL_HIST, POOL_WIDTH), _F32)
        zc = jnp.zeros((nbp, CONV_HIST, CONV_DIM), _F32)
        zh = jnp.zeros((nbp, SSD_HEADS, SSD_HEAD_DIM, SSD_STATE), _F32)
        xp, npool, nconv, nssm = _mixer(xp, mod_p, zp, zc, zh, mixer_w, pos0=0, tl=prompt_tl)
        xp = _ffn(xp, mod_p, ffn_w + (fw,), tb=1, tl=ffn_tl)
        outs["pp"].append(npool); outs["cp"].append(nconv); outs["sp"].append(nssm)
        xs, npool, nconv, nssm = _mixer(xs, mod_s, cache_pool[l], cache_conv[l], state_ssm[l], mixer_w,
                                        pos0=past_len, tl=seq_s)
        xs = _ffn(xs, mod_s, ffn_w + (fw,), tb=nbs, tl=seq_s)
        outs["ps"].append(npool); outs["cs"].append(nconv); outs["ss"].append(nssm)
    st = lambda k: jnp.stack(outs[k], 0)
    return (xp, xs, st("pp"), st("cp"), st("sp"), st("ps"), st("cs"), st("ss"))


def kernel(x_prompt, x_sample, cache_pool, cache_conv, state_ssm, c_prompt, c_sample, norm_mix_w, norm_ffn_w, w_ada, b_ada, w_in, pool_w, pool_b, pool_scale, conv_w, conv_b, dt_bias, a_log, d_skip, ssd_norm_w, w_out, w_ff1, b_ff1, w_ff2, b_ff2, final_norm_w):
    return _run(x_prompt, x_sample, cache_pool, cache_conv, state_ssm, c_prompt, c_sample,
                norm_mix_w, norm_ffn_w, w_ada, b_ada, w_in, pool_w, pool_b, pool_scale,
                conv_w, conv_b, dt_bias, a_log, d_skip, ssd_norm_w, w_out,
                w_ff1, b_ff1, w_ff2, b_ff2, final_norm_w,
                prompt_tl=256, ffn_tl=512, past_len=4096)
```

```python
import functools

import jax
import jax.numpy as jnp
from jax import lax
from jax.experimental import pallas as pl
from jax.experimental.pallas import tpu as pltpu

D_MODEL = 1024
POOL_WIDTH = 1024
POOL_GROUPS = 4
POOL_GROUP_DIM = POOL_WIDTH // POOL_GROUPS
POOL_WINDOWS = (2, 4, 8, 16)
POOL_HIST = 15
POOL_HIST_ROWS = 16
SSD_WIDTH = 1024
SSD_HEAD_DIM = 64
SSD_HEADS = 16
SSD_GROUPS = 2
SSD_GROUP_WIDTH = SSD_WIDTH // SSD_GROUPS
SSD_STATE = 128
SSD_CONV = 4
CONV_HIST = SSD_CONV - 1
CONV_HIST_ROWS = 8
CONV_DIM = SSD_WIDTH + 2 * SSD_GROUPS * SSD_STATE
D_FF = 4 * D_MODEL
EPS = 1e-6

V7X_VMEM_LIMIT_BYTES = 56 * 1024 * 1024
LANES = 128
LOG2_E = 1.4426950408889634
SSD_CHUNK = 128
PROJ_COL_BLOCK = 256
WORK_NORM = 0
WORK_POOL_GROUP = (130, 270, 400, 400)
WORK_CONV_BLOCK = 220
WORK_SSD_PRELUDE = 300
WORK_SSD_HEAD_PAIR = 130
WORK_SSD_TAIL = 350
WORK_GATE = 400
SIDE_PER_MAIN = 0.95

_BF16 = jnp.bfloat16
_F32 = jnp.float32


def _dot(a, b):
    return jnp.dot(a, b, preferred_element_type=_F32)


def _dot_nt(a, b):
    return lax.dot_general(a, b, (((1,), (1,)), ((), ())), preferred_element_type=_F32)


def _dot_tn(a, b):
    return lax.dot_general(a, b, (((0,), (0,)), ((), ())), preferred_element_type=_F32)


def _split3(v):
    v1 = v.astype(_BF16)
    r1 = v - v1.astype(_F32)
    v2 = r1.astype(_BF16)
    v3 = (r1 - v2.astype(_F32)).astype(_BF16)
    return v1, v2, v3


def _split2(v):
    v1 = v.astype(_BF16)
    v2 = (v - v1.astype(_F32)).astype(_BF16)
    return v1, v2


def _silu(v):
    hv = 0.5 * v
    return hv + hv * jnp.tanh(hv)


def _softplus(v):
    return jnp.maximum(v, 0.0) + jnp.log1p(jnp.exp(-jnp.abs(v)))


def _rms(v):
    return v * lax.rsqrt(jnp.mean(v * v, axis=-1, keepdims=True) + EPS)


def _shift_rows(v, k):
    return pltpu.roll(v, k, axis=0)


def _ada_kernel(c_ref, w_ref, b_ref, o_ref):
    s = _silu(c_ref[...]).astype(_BF16)
    o_ref[...] = _dot(s, w_ref[...].astype(_BF16)) + b_ref[...]


def _ada(c_all, w_ada, b_ada):
    nb = c_all.shape[0]
    n_out = w_ada.shape[1]
    blk = D_MODEL
    return pl.pallas_call(
        _ada_kernel,
        grid=(n_out // blk,),
        in_specs=[
            pl.BlockSpec((nb, D_MODEL), lambda j: (0, 0)),
            pl.BlockSpec((D_MODEL, blk), lambda j: (0, j)),
            pl.BlockSpec((1, blk), lambda j: (0, j)),
        ],
        out_specs=pl.BlockSpec((nb, blk), lambda j: (0, j)),
        out_shape=jax.ShapeDtypeStruct((nb, n_out), _F32),
        compiler_params=pltpu.CompilerParams(dimension_semantics=("arbitrary",)),
        name="ada",
    )(c_all, w_ada, b_ada.reshape(1, n_out))


def _interleave(main, side, side_per_main):
    main_done = 0.0
    side_done = 0.0
    side_left = True
    for work in main:
        main_done += work
        while side_left and side_done < main_done * side_per_main:
            cost = next(side, None)
            side_left = cost is not None
            side_done += cost or 0.0
    for _ in side:
        pass


def _ssd_chunk(xs, bm, cm, dtr, st_ref, dtb, neg_a, dskip, out):
    q = xs.shape[0]
    dt_c = _softplus(dtr + dtb)
    a_c = dt_c * neg_a
    ri = lax.broadcasted_iota(jnp.int32, (q, q), 0)
    ci = lax.broadcasted_iota(jnp.int32, (q, q), 1)
    causal = ri >= ci
    tri = jnp.where(causal, 1.0, 0.0).astype(_BF16)
    a1, a2, a3 = _split3(a_c)
    acs_c = _dot(jnp.concatenate([tri, tri, tri], axis=1),
                 jnp.concatenate([a1, a2, a3], axis=0))

    side = max(q, LANES)
    acs2_c = acs_c * LOG2_E
    cols = jnp.concatenate([dt_c, acs2_c, jnp.zeros((q, LANES - 2 * SSD_HEADS), _F32)], axis=1)
    if q < side:
        cols = jnp.concatenate([cols, jnp.zeros((side - q, LANES), _F32)], axis=0)
    rows = cols.T
    dt_r = rows[0:SSD_HEADS, 0:q]
    acs2_r = rows[SSD_HEADS:2 * SSD_HEADS, 0:q]

    hi = lax.broadcasted_iota(jnp.int32, (2 * SSD_HEADS, SSD_WIDTH), 0) % SSD_HEADS
    chn = lax.broadcasted_iota(jnp.int32, (2 * SSD_HEADS, SSD_WIDTH), 1)
    expand2 = jnp.where((chn >= hi * SSD_HEAD_DIM) & (chn < (hi + 1) * SSD_HEAD_DIM),
                        1.0, 0.0).astype(_BF16)

    def _expand_heads(v):
        return _dot(jnp.concatenate(_split2(v), axis=1), expand2)

    eacs_rep = _expand_heads(jnp.exp(acs_c))
    w_state_rep = _expand_heads(dt_c * jnp.exp(acs_c[q - 1:q, :] - acs_c))
    chunk_decay_rep = eacs_rep[q - 1:q, :]

    lane = lax.broadcasted_iota(jnp.int32, (q, 2 * SSD_HEAD_DIM), 1)
    first_head = lane < SSD_HEAD_DIM
    bgs = [bm[:, g * SSD_STATE:(g + 1) * SSD_STATE].astype(_BF16) for g in range(SSD_GROUPS)]
    cgs = [cm[:, g * SSD_STATE:(g + 1) * SSD_STATE].astype(_BF16) for g in range(SSD_GROUPS)]
    cbm = [jnp.where(causal, _dot_nt(cgs[g], bgs[g]), 0.0) for g in range(SSD_GROUPS)]
    yield WORK_SSD_PRELUDE
    yd_parts = []
    for p in range(SSD_HEADS // 2):
        g = (2 * p) // (SSD_HEADS // SSD_GROUPS)
        ms = []
        for hh in (2 * p, 2 * p + 1):
            seg2 = jnp.minimum(acs2_c[:, hh:hh + 1] - acs2_r[hh:hh + 1, :], 0.0)
            ms.append((cbm[g] * jnp.exp2(seg2) * dt_r[hh:hh + 1, :]).astype(_BF16))
        xp = xs[:, p * 2 * SSD_HEAD_DIM:(p + 1) * 2 * SSD_HEAD_DIM]
        rhs = jnp.concatenate([jnp.where(first_head, xp, 0.0), jnp.where(first_head, 0.0, xp)],
                              axis=0).astype(_BF16)
        yd_parts.append(_dot(jnp.concatenate(ms, axis=1), rhs))
        yield WORK_SSD_HEAD_PAIR
    y_diag = jnp.concatenate(yd_parts, axis=1)

    xw = (xs * w_state_rep).astype(_BF16)
    yo_parts = []
    for g in range(SSD_GROUPS):
        gc = slice(g * SSD_GROUP_WIDTH, (g + 1) * SSD_GROUP_WIDTH)
        st_g = st_ref[:, gc]
        yo_parts.append(_dot(cgs[g], st_g.astype(_BF16)))
        st_ref[:, gc] = st_g * chunk_decay_rep[:, gc] + _dot_tn(bgs[g], xw[:, gc])
    y_off = jnp.concatenate(yo_parts, axis=1) * eacs_rep
    out.append(y_diag + y_off + dskip * xs)
    yield WORK_SSD_TAIL


class _Stage:
    def __init__(self, ubuf, z, cbuf, dtr, y):
        self.ubuf = ubuf
        self.z = z
        self.cbuf = cbuf
        self.dtr = dtr
        self.y = y


def _mixer_kernel(xa_ref, xc_ref, moda_ref, modc_ref, ph_ref, ch_ref, h0_ref,
                  nw_ref, wu_ref, wz_ref, wxbc_ref, wdt_ref,
                  pw_ref, pb_ref, ps_ref, cw_ref, cb_ref,
                  dtb_ref, alog_ref, dskip_ref, snw_ref, wout_ref,
                  x1_ref, npool_ref, nconv_ref, nssm_ref,
                  u0s, z0s, c0s, d0s, y0s, u1s, z1s, c1s, d1s, y1s, st_ref, hb_ref,
                  *, tl, q, pos0, n_steps, n_tiles):
    i = pl.program_id(0)
    b_step = lax.rem(i - 1, n_steps)
    stages = (_Stage(u0s, z0s, c0s, d0s, y0s), _Stage(u1s, z1s, c1s, d1s, y1s))

    @pl.when(i == 0)
    def _zero_first_stage():
        for ref in (u1s, z1s, c1s, d1s, y1s, st_ref):
            ref[...] = jnp.zeros(ref.shape, ref.dtype)

    def phase_a(sa):
        blk = PROJ_COL_BLOCK
        cost = tl * blk // 256
        for c0 in range(0, POOL_WIDTH, blk):
            sa.ubuf[POOL_HIST_ROWS:POOL_HIST_ROWS + tl, c0:c0 + blk] = \
                _dot(hb_ref[...], wu_ref[:, c0:c0 + blk])
            yield cost
        for c0 in range(0, CONV_DIM, blk):
            sa.cbuf[CONV_HIST_ROWS:CONV_HIST_ROWS + tl, c0:c0 + blk] = \
                _dot(hb_ref[...], wxbc_ref[:, c0:c0 + blk])
            yield cost
        sa.dtr[...] = _dot(hb_ref[...], wdt_ref[...])
        for c0 in range(0, SSD_WIDTH, blk):
            sa.z[:, c0:c0 + blk] = _dot(hb_ref[...], wz_ref[:, c0:c0 + blk])
            yield cost

    def phase_c(sc):
        blk = PROJ_COL_BLOCK
        g1 = modc_ref[0][2:3]
        for c0 in range(0, D_MODEL, blk):
            mix = _dot(sc.y[...], wout_ref[:, c0:c0 + blk])
            x1_ref[0, :, c0:c0 + blk] = xc_ref[0, :, c0:c0 + blk] + g1[:, c0:c0 + blk] * mix
            yield 2 * tl * blk // 256

    def side_pieces(sa, sc):
        a_pieces, c_pieces = phase_a(sa), phase_c(sc)
        for k, cost in enumerate(a_pieces):
            yield cost
            if k % 4 == 2:
                c_cost = next(c_pieces, None)
                if c_cost is not None:
                    yield c_cost
        yield from c_pieces

    def phase_b(sb, sa):
        ts = tl / 256.0
        yield WORK_NORM * ts

        pos = (lax.broadcasted_iota(jnp.int32, (tl, POOL_GROUP_DIM), 0)
               + (pos0 + b_step * tl)).astype(_F32)
        ya_parts = []
        for g, w in enumerate(POOL_WINDOWS):
            lo = g * POOL_GROUP_DIM
            ug = sb.ubuf[:, lo:lo + POOL_GROUP_DIM]
            wsum = ug
            k = 1
            while k < w:
                wsum = wsum + _shift_rows(wsum, k)
                k *= 2
            cnt = jnp.minimum(jnp.float32(w), pos + 1.0)
            pooled = wsum[POOL_HIST_ROWS:, :] / cnt - ug[POOL_HIST_ROWS:, :]
            ya_parts.append(_dot(pooled.astype(_BF16), pw_ref[g]))
            yield WORK_POOL_GROUP[g] * ts
        ya = (jnp.concatenate(ya_parts, axis=1) + pb_ref[...]) * ps_ref[...]

        act_parts = []
        for c0 in range(0, CONV_DIM, PROJ_COL_BLOCK):
            cc = slice(c0, c0 + PROJ_COL_BLOCK)
            full = sb.cbuf[:, cc]
            prev = _shift_rows(full, 1)
            near = full * cw_ref[3:4, cc] + prev * cw_ref[2:3, cc]
            far = full * cw_ref[1:2, cc] + prev * cw_ref[0:1, cc]
            conv = cb_ref[:, cc] + near + _shift_rows(far, 2)
            act_parts.append(_silu(conv[CONV_HIST_ROWS:, :]))
            yield WORK_CONV_BLOCK * ts
        xbc_act = jnp.concatenate(act_parts, axis=1)
        b_off = SSD_WIDTH
        c_off = SSD_WIDTH + SSD_GROUPS * SSD_STATE

        neg_a = -jnp.exp(alog_ref[...])
        dtr = sb.dtr[...]
        y_parts = []
        for c in range(tl // q):
            r = slice(c * q, (c + 1) * q)
            yield from _ssd_chunk(xbc_act[r, 0:SSD_WIDTH], xbc_act[r, b_off:c_off],
                                  xbc_act[r, c_off:CONV_DIM], dtr[r, :], st_ref,
                                  dtb_ref[...], neg_a, dskip_ref[...], y_parts)
        y = jnp.concatenate(y_parts, axis=0) if len(y_parts) > 1 else y_parts[0]

        y = y * _silu(sb.z[...])
        yb = jnp.concatenate(
            [_rms(y[:, g * SSD_GROUP_WIDTH:(g + 1) * SSD_GROUP_WIDTH]) for g in range(SSD_GROUPS)],
            axis=1) * snw_ref[...]
        sa.y[:, 0:POOL_WIDTH] = ya.astype(_BF16)
        sa.y[:, POOL_WIDTH:POOL_WIDTH + SSD_WIDTH] = yb.astype(_BF16)

        sa.ubuf[0:POOL_HIST_ROWS, :] = sb.ubuf[tl:tl + POOL_HIST_ROWS, :]
        sa.cbuf[0:CONV_HIST_ROWS, :] = sb.cbuf[tl:tl + CONV_HIST_ROWS, :]
        yield WORK_GATE * ts

    for parity in (0, 1):
        @pl.when(lax.rem(i, 2) == parity)
        def _step(parity=parity):
            sa, sb = stages[parity], stages[1 - parity]

            @pl.when(b_step == 0)
            def _load_history():
                sb.ubuf[0:POOL_HIST_ROWS, :] = ph_ref[0]
                sb.cbuf[0:CONV_HIST_ROWS, :] = ch_ref[0]
                st_ref[...] = h0_ref[0].reshape(SSD_WIDTH, SSD_STATE).T

            x = xa_ref[0]
            mod = moda_ref[0]
            sh1, sc1 = mod[0:1], mod[1:2]
            h = _rms(x) * (nw_ref[...] * (1.0 + sc1)) + sh1
            hb_ref[...] = h.astype(_BF16)

            _interleave(phase_b(sb, sa), side_pieces(sa, sb), SIDE_PER_MAIN)

            @pl.when((b_step == n_steps - 1) & (i <= n_tiles))
            def _emit_caches():
                npool_ref[0] = sb.ubuf[tl:tl + POOL_HIST_ROWS, :]
                nconv_ref[0] = sb.cbuf[tl:tl + CONV_HIST_ROWS, :]
                nssm_ref[0] = st_ref[...].T.reshape(SSD_HEADS, SSD_HEAD_DIM, SSD_STATE)


def _mixer(x, mod, pool_hist, conv_hist, h0, wts, *, pos0, tl):
    nb, seq, _ = x.shape
    assert seq % tl == 0
    n_steps = seq // tl
    n_tiles = nb * n_steps
    q = min(tl, SSD_CHUNK)
    assert tl % q == 0
    ph = jnp.pad(pool_hist, ((0, 0), (POOL_HIST_ROWS - POOL_HIST, 0), (0, 0)))
    ch = jnp.pad(conv_hist, ((0, 0), (CONV_HIST_ROWS - CONV_HIST, 0), (0, 0)))

    def const(shape):
        return pl.BlockSpec(shape, lambda i: (0,) * len(shape))

    def tile_a(i):
        return jnp.minimum(i, n_tiles - 1)

    def tile_b(i):
        return jnp.clip(i - 1, 0, n_tiles - 1)

    def tile_c(i):
        return jnp.maximum(i - 2, 0)

    in_specs = [
        pl.BlockSpec((1, tl, D_MODEL), lambda i: (tile_a(i) // n_steps, tile_a(i) % n_steps, 0)),
        pl.BlockSpec((1, tl, D_MODEL), lambda i: (tile_c(i) // n_steps, tile_c(i) % n_steps, 0)),
        pl.BlockSpec((1, 6, D_MODEL), lambda i: (tile_a(i) // n_steps, 0, 0)),
        pl.BlockSpec((1, 6, D_MODEL), lambda i: (tile_c(i) // n_steps, 0, 0)),
        pl.BlockSpec((1, POOL_HIST_ROWS, POOL_WIDTH), lambda i: (tile_b(i) // n_steps, 0, 0)),
        pl.BlockSpec((1, CONV_HIST_ROWS, CONV_DIM), lambda i: (tile_b(i) // n_steps, 0, 0)),
        pl.BlockSpec((1, SSD_HEADS, SSD_HEAD_DIM, SSD_STATE), lambda i: (tile_b(i) // n_steps, 0, 0, 0)),
        const((1, D_MODEL)),
        const((D_MODEL, POOL_WIDTH)),
        const((D_MODEL, SSD_WIDTH)),
        const((D_MODEL, CONV_DIM)),
        const((D_MODEL, SSD_HEADS)),
        const((POOL_GROUPS, POOL_GROUP_DIM, POOL_GROUP_DIM)),
        const((1, POOL_WIDTH)),
        const((1, POOL_WIDTH)),
        const((SSD_CONV, CONV_DIM)),
        const((1, CONV_DIM)),
        const((1, SSD_HEADS)),
        const((1, SSD_HEADS)),
        const((1, SSD_WIDTH)),
        const((1, SSD_WIDTH)),
        const((POOL_WIDTH + SSD_WIDTH, D_MODEL)),
    ]
    out_specs = [
        pl.BlockSpec((1, tl, D_MODEL), lambda i: (tile_c(i) // n_steps, tile_c(i) % n_steps, 0)),
        pl.BlockSpec((1, POOL_HIST_ROWS, POOL_WIDTH), lambda i: (tile_b(i) // n_steps, 0, 0)),
        pl.BlockSpec((1, CONV_HIST_ROWS, CONV_DIM), lambda i: (tile_b(i) // n_steps, 0, 0)),
        pl.BlockSpec((1, SSD_HEADS, SSD_HEAD_DIM, SSD_STATE), lambda i: (tile_b(i) // n_steps, 0, 0, 0)),
    ]
    out_shape = [
        jax.ShapeDtypeStruct((nb, seq, D_MODEL), _F32),
        jax.ShapeDtypeStruct((nb, POOL_HIST_ROWS, POOL_WIDTH), _F32),
        jax.ShapeDtypeStruct((nb, CONV_HIST_ROWS, CONV_DIM), _F32),
        jax.ShapeDtypeStruct((nb, SSD_HEADS, SSD_HEAD_DIM, SSD_STATE), _F32),
    ]
    x1, npool, nconv, nssm = pl.pallas_call(
        functools.partial(_mixer_kernel, tl=tl, q=q, pos0=pos0, n_steps=n_steps, n_tiles=n_tiles),
        grid=(n_tiles + 2,),
        in_specs=in_specs,
        out_specs=out_specs,
        out_shape=out_shape,
        scratch_shapes=2 * [
            pltpu.VMEM((POOL_HIST_ROWS + tl, POOL_WIDTH), _F32),
            pltpu.VMEM((tl, SSD_WIDTH), _F32),
            pltpu.VMEM((CONV_HIST_ROWS + tl, CONV_DIM), _F32),
            pltpu.VMEM((tl, SSD_HEADS), _F32),
            pltpu.VMEM((tl, POOL_WIDTH + SSD_WIDTH), _BF16),
        ] + [pltpu.VMEM((SSD_STATE, SSD_WIDTH), _F32), pltpu.VMEM((tl, D_MODEL), _BF16)],
        compiler_params=pltpu.CompilerParams(
            dimension_semantics=("arbitrary",),
            vmem_limit_bytes=V7X_VMEM_LIMIT_BYTES),
        name="mixer",
    )(x, x, mod, mod, ph, ch, h0, *wts)
    return x1, npool[:, POOL_HIST_ROWS - POOL_HIST:], nconv[:, CONV_HIST_ROWS - CONV_HIST:], nssm


def _ffn_kernel(x_ref, mod_ref, nw_ref, w1_ref, b1_ref, w2_ref, b2_ref, fw_ref, y_ref, *, ff_chunk):
    tb, tl, _ = x_ref.shape
    x3 = x_ref[...]
    mod = mod_ref[...]
    sh2, sc2, g2 = mod[:, 3:4, :], mod[:, 4:5, :], mod[:, 5:6, :]
    h3 = (_rms(x3) * nw_ref[...]) * (1.0 + sc2) + sh2
    hb = h3.reshape(tb * tl, D_MODEL).astype(_BF16)
    acc = None
    for j in range(D_FF // ff_chunk):
        cols = slice(j * ff_chunk, (j + 1) * ff_chunk)
        f = jnp.maximum(_dot(hb, w1_ref[:, cols]) + b1_ref[:, cols], 0.0)
        part = _dot((f * f).astype(_BF16), w2_ref[cols, :])
        acc = part if acc is None else acc + part
    f2 = (acc + b2_ref[...]).reshape(tb, tl, D_MODEL)
    x2 = x3 + g2 * f2
    y_ref[...] = _rms(x2) * fw_ref[...]


def _ffn(x1, mod, wts, *, tb, tl):
    nb, seq, _ = x1.shape
    assert nb % tb == 0 and seq % tl == 0

    def const(shape):
        return pl.BlockSpec(shape, lambda b, s: (0,) * len(shape))

    return pl.pallas_call(
        functools.partial(_ffn_kernel, ff_chunk=1024),
        grid=(nb // tb, seq // tl),
        in_specs=[
            pl.BlockSpec((tb, tl, D_MODEL), lambda b, s: (b, s, 0)),
            pl.BlockSpec((tb, 6, D_MODEL), lambda b, s: (b, 0, 0)),
            const((1, D_MODEL)),
            const((D_MODEL, D_FF)),
            const((1, D_FF)),
            const((D_FF, D_MODEL)),
            const((1, D_MODEL)),
            const((1, D_MODEL)),
        ],
        out_specs=pl.BlockSpec((tb, tl, D_MODEL), lambda b, s: (b, s, 0)),
        out_shape=jax.ShapeDtypeStruct((nb, seq, D_MODEL), _F32),
        compiler_params=pltpu.CompilerParams(
            dimension_semantics=("arbitrary", "arbitrary"),
            vmem_limit_bytes=V7X_VMEM_LIMIT_BYTES),
        name="ffn",
    )(x1, mod, *wts)


def _layer_weights(l, norm_mix_w, norm_ffn_w, w_in, pool_w, pool_b, pool_scale, conv_w, conv_b,
                   dt_bias, a_log, d_skip, ssd_norm_w, w_out, w_ff1, b_ff1, w_ff2, b_ff2):
    win = w_in[l]
    o_z = POOL_WIDTH
    o_xbc = POOL_WIDTH + SSD_WIDTH
    o_dt = o_xbc + CONV_DIM
    mixer_w = (
        norm_mix_w[l].reshape(1, D_MODEL),
        win[:, :o_z].astype(_BF16),
        win[:, o_z:o_xbc].astype(_BF16),
        win[:, o_xbc:o_dt].astype(_BF16),
        win[:, o_dt:o_dt + SSD_HEADS].astype(_BF16),
        pool_w[l].astype(_BF16),
        pool_b[l].reshape(1, POOL_WIDTH),
        pool_scale[l].reshape(1, POOL_WIDTH),
        conv_w[l],
        conv_b[l].reshape(1, CONV_DIM),
        dt_bias[l].reshape(1, SSD_HEADS),
        a_log[l].reshape(1, SSD_HEADS),
        jnp.repeat(d_skip[l], SSD_HEAD_DIM).reshape(1, SSD_WIDTH),
        ssd_norm_w[l].reshape(1, SSD_WIDTH),
        w_out[l].astype(_BF16),
    )
    ffn_w = (
        norm_ffn_w[l].reshape(1, D_MODEL),
        w_ff1[l].astype(_BF16),
        b_ff1[l].reshape(1, D_FF),
        w_ff2[l].astype(_BF16),
        b_ff2[l].reshape(1, D_MODEL),
    )
    return mixer_w, ffn_w


def _run(x_prompt, x_sample, cache_pool, cache_conv, state_ssm, c_prompt, c_sample,
         norm_mix_w, norm_ffn_w, w_ada, b_ada, w_in, pool_w, pool_b, pool_scale,
         conv_w, conv_b, dt_bias, a_log, d_skip, ssd_norm_w, w_out,
         w_ff1, b_ff1, w_ff2, b_ff2, final_norm_w, *, prompt_tl, ffn_tl, past_len):
    depth = w_in.shape[0]
    assert depth == 1, "the final norm is fused into the (single) layer's ffn call"
    nbp, seq_p, _ = x_prompt.shape
    nbs, seq_s, _ = x_sample.shape
    fw = final_norm_w.reshape(1, D_MODEL)
    c_all = jnp.concatenate([c_prompt, c_sample], axis=0)
    xp, xs = x_prompt, x_sample
    outs = {k: [] for k in ("pp", "cp", "sp", "ps", "cs", "ss")}
    for l in range(depth):
        mod = _ada(c_all, w_ada[l], b_ada[l]).reshape(nbp + nbs, 6, D_MODEL)
        mod_p, mod_s = mod[:nbp], mod[nbp:]
        mixer_w, ffn_w = _layer_weights(l, norm_mix_w, norm_ffn_w, w_in, pool_w, pool_b, pool_scale,
                                        conv_w, conv_b, dt_bias, a_log, d_skip, ssd_norm_w, w_out,
                                        w_ff1, b_ff1, w_ff2, b_ff2)
        zp = jnp.zeros((nbp, POOL_HIST, POOL_WIDTH), _F32)
        zc = jnp.zeros((nbp, CONV_HIST, CONV_DIM), _F32)
        zh = jnp.zeros((nbp, SSD_HEADS, SSD_HEAD_DIM, SSD_STATE), _F32)
        xp, npool, nconv, nssm = _mixer(xp, mod_p, zp, zc, zh, mixer_w, pos0=0, tl=prompt_tl)
        xp = _ffn(xp, mod_p, ffn_w + (fw,), tb=1, tl=ffn_tl)
        outs["pp"].append(npool); outs["cp"].append(nconv); outs["sp"].append(nssm)
        xs, npool, nconv, nssm = _mixer(xs, mod_s, cache_pool[l], cache_conv[l], state_ssm[l], mixer_w,
                                        pos0=past_len, tl=seq_s)
        xs = _ffn(xs, mod_s, ffn_w + (fw,), tb=nbs, tl=seq_s)
        outs["ps"].append(npool); outs["cs"].append(nconv); outs["ss"].append(nssm)
    st = lambda k: jnp.stack(outs[k], 0)
    return (xp, xs, st("pp"), st("cp"), st("sp"), st("ps"), st("cs"), st("ss"))


def kernel(x_prompt, x_sample, cache_pool, cache_conv, state_ssm, c_prompt, c_sample, norm_mix_w, norm_ffn_w, w_ada, b_ada, w_in, pool_w, pool_b, pool_scale, conv_w, conv_b, dt_bias, a_log, d_skip, ssd_norm_w, w_out, w_ff1, b_ff1, w_ff2, b_ff2, final_norm_w):
    return _run(x_prompt, x_sample, cache_pool, cache_conv, state_ssm, c_prompt, c_sample,
                norm_mix_w, norm_ffn_w, w_ada, b_ada, w_in, pool_w, pool_b, pool_scale,
                conv_w, conv_b, dt_bias, a_log, d_skip, ssd_norm_w, w_out,
                w_ff1, b_ff1, w_ff2, b_ff2, final_norm_w,
                prompt_tl=512, ffn_tl=512, past_len=4096)
```

```python
import functools

import jax
import jax.numpy as jnp
from jax import lax
from jax.experimental import pallas as pl
from jax.experimental.pallas import tpu as pltpu

D_MODEL = 1024
POOL_WIDTH = 1024
POOL_GROUPS = 4
POOL_GROUP_DIM = POOL_WIDTH // POOL_GROUPS
POOL_WINDOWS = (2, 4, 8, 16)
POOL_HIST = 15
POOL_HIST_ROWS = 16
SSD_WIDTH = 1024
SSD_HEAD_DIM = 64
SSD_HEADS = 16
SSD_GROUPS = 2
SSD_GROUP_WIDTH = SSD_WIDTH // SSD_GROUPS
SSD_STATE = 128
SSD_CONV = 4
CONV_HIST = SSD_CONV - 1
CONV_HIST_ROWS = 8
CONV_DIM = SSD_WIDTH + 2 * SSD_GROUPS * SSD_STATE
D_FF = 4 * D_MODEL
EPS = 1e-6

V7X_VMEM_LIMIT_BYTES = 56 * 1024 * 1024
LANES = 128
LOG2_E = 1.4426950408889634
SSD_CHUNK = 128
PROJ_COL_BLOCK = 256
WORK_NORM = 0
WORK_POOL_GROUP = (130, 270, 400, 400)
WORK_CONV_BLOCK = 220
WORK_SSD_PRELUDE = 300
WORK_SSD_HEAD_PAIR = 130
WORK_SSD_TAIL = 350
WORK_GATE = 400
SIDE_PER_MAIN = 0.95

_BF16 = jnp.bfloat16
_F32 = jnp.float32


def _dot(a, b):
    return jnp.dot(a, b, preferred_element_type=_F32)


def _dot_nt(a, b):
    return lax.dot_general(a, b, (((1,), (1,)), ((), ())), preferred_element_type=_F32)


def _dot_tn(a, b):
    return lax.dot_general(a, b, (((0,), (0,)), ((), ())), preferred_element_type=_F32)


def _split3(v):
    v1 = v.astype(_BF16)
    r1 = v - v1.astype(_F32)
    v2 = r1.astype(_BF16)
    v3 = (r1 - v2.astype(_F32)).astype(_BF16)
    return v1, v2, v3


def _split2(v):
    v1 = v.astype(_BF16)
    v2 = (v - v1.astype(_F32)).astype(_BF16)
    return v1, v2


def _silu(v):
    hv = 0.5 * v
    return hv + hv * jnp.tanh(hv)


def _softplus(v):
    return jnp.maximum(v, 0.0) + jnp.log1p(jnp.exp(-jnp.abs(v)))


def _rms(v):
    return v * lax.rsqrt(jnp.mean(v * v, axis=-1, keepdims=True) + EPS)


def _shift_rows(v, k):
    return pltpu.roll(v, k, axis=0)


def _ada_kernel(c_ref, w_ref, b_ref, o_ref):
    s = _silu(c_ref[...]).astype(_BF16)
    o_ref[...] = _dot(s, w_ref[...].astype(_BF16)) + b_ref[...]


def _ada(c_all, w_ada, b_ada):
    nb = c_all.shape[0]
    n_out = w_ada.shape[1]
    blk = D_MODEL
    return pl.pallas_call(
        _ada_kernel,
        grid=(n_out // blk,),
        in_specs=[
            pl.BlockSpec((nb, D_MODEL), lambda j: (0, 0)),
            pl.BlockSpec((D_MODEL, blk), lambda j: (0, j)),
            pl.BlockSpec((1, blk), lambda j: (0, j)),
        ],
        out_specs=pl.BlockSpec((nb, blk), lambda j: (0, j)),
        out_shape=jax.ShapeDtypeStruct((nb, n_out), _F32),
        compiler_params=pltpu.CompilerParams(dimension_semantics=("arbitrary",)),
        name="ada",
    )(c_all, w_ada, b_ada.reshape(1, n_out))


def _interleave(main, side, side_per_main):
    main_done = 0.0
    side_done = 0.0
    side_left = True
    for work in main:
        main_done += work
        while side_left and side_done < main_done * side_per_main:
            cost = next(side, None)
            side_left = cost is not None
            side_done += cost or 0.0
    for _ in side:
        pass


def _ssd_decay_terms(dtr, dtb, neg_a):
    q = dtr.shape[0]
    dt_c = _softplus(dtr + dtb)
    a_c = dt_c * neg_a
    ri = lax.broadcasted_iota(jnp.int32, (q, q), 0)
    ci = lax.broadcasted_iota(jnp.int32, (q, q), 1)
    tri = jnp.where(ri >= ci, 1.0, 0.0).astype(_BF16)
    a1, a2, a3 = _split3(a_c)
    acs_c = _dot(jnp.concatenate([tri, tri, tri], axis=1),
                 jnp.concatenate([a1, a2, a3], axis=0))

    side = max(q, LANES)
    acs2_c = acs_c * LOG2_E
    cols = jnp.concatenate([dt_c, acs2_c, jnp.zeros((q, LANES - 2 * SSD_HEADS), _F32)], axis=1)
    if q < side:
        cols = jnp.concatenate([cols, jnp.zeros((side - q, LANES), _F32)], axis=0)
    rows = cols.T
    dt_r = rows[0:SSD_HEADS, 0:q]
    acs2_r = rows[SSD_HEADS:2 * SSD_HEADS, 0:q]
    eacs = jnp.exp(acs_c)
    w_state = dt_c * jnp.exp(acs_c[q - 1:q, :] - acs_c)
    return dt_r, acs2_c, acs2_r, eacs, w_state


def _ssd_chunk(xs, bm, cm, decay_terms, st_ref, dskip, out):
    q = xs.shape[0]
    dt_r, acs2_c, acs2_r, eacs, w_state = decay_terms

    hi = lax.broadcasted_iota(jnp.int32, (2 * SSD_HEADS, SSD_WIDTH), 0) % SSD_HEADS
    chn = lax.broadcasted_iota(jnp.int32, (2 * SSD_HEADS, SSD_WIDTH), 1)
    expand2 = jnp.where((chn >= hi * SSD_HEAD_DIM) & (chn < (hi + 1) * SSD_HEAD_DIM),
                        1.0, 0.0).astype(_BF16)

    def _expand_heads(v):
        return _dot(jnp.concatenate(_split2(v), axis=1), expand2)

    eacs_rep = _expand_heads(eacs)
    w_state_rep = _expand_heads(w_state)
    chunk_decay_rep = eacs_rep[q - 1:q, :]
    ri = lax.broadcasted_iota(jnp.int32, (q, q), 0)
    ci = lax.broadcasted_iota(jnp.int32, (q, q), 1)
    causal = ri >= ci

    lane = lax.broadcasted_iota(jnp.int32, (q, 2 * SSD_HEAD_DIM), 1)
    first_head = lane < SSD_HEAD_DIM
    bgs = [bm[:, g * SSD_STATE:(g + 1) * SSD_STATE].astype(_BF16) for g in range(SSD_GROUPS)]
    cgs = [cm[:, g * SSD_STATE:(g + 1) * SSD_STATE].astype(_BF16) for g in range(SSD_GROUPS)]
    cbm = [jnp.where(causal, _dot_nt(cgs[g], bgs[g]), 0.0) for g in range(SSD_GROUPS)]
    yield WORK_SSD_PRELUDE
    yd_parts = []
    for p in range(SSD_HEADS // 2):
        g = (2 * p) // (SSD_HEADS // SSD_GROUPS)
        ms = []
        for hh in (2 * p, 2 * p + 1):
            seg2 = jnp.minimum(acs2_c[:, hh:hh + 1] - acs2_r[hh:hh + 1, :], 0.0)
            ms.append((cbm[g] * jnp.exp2(seg2) * dt_r[hh:hh + 1, :]).astype(_BF16))
        xp = xs[:, p * 2 * SSD_HEAD_DIM:(p + 1) * 2 * SSD_HEAD_DIM]
        rhs = jnp.concatenate([jnp.where(first_head, xp, 0.0), jnp.where(first_head, 0.0, xp)],
                              axis=0).astype(_BF16)
        yd_parts.append(_dot(jnp.concatenate(ms, axis=1), rhs))
        yield WORK_SSD_HEAD_PAIR
    y_diag = jnp.concatenate(yd_parts, axis=1)

    xw = (xs * w_state_rep).astype(_BF16)
    yo_parts = []
    for g in range(SSD_GROUPS):
        gc = slice(g * SSD_GROUP_WIDTH, (g + 1) * SSD_GROUP_WIDTH)
        st_g = st_ref[:, gc]
        yo_parts.append(_dot(cgs[g], st_g.astype(_BF16)))
        st_ref[:, gc] = st_g * chunk_decay_rep[:, gc] + _dot_tn(bgs[g], xw[:, gc])
    y_off = jnp.concatenate(yo_parts, axis=1) * eacs_rep
    out.append(y_diag + y_off + dskip * xs)
    yield WORK_SSD_TAIL


class _Stage:
    def __init__(self, ubuf, z, cbuf, dtr, y):
        self.ubuf = ubuf
        self.z = z
        self.cbuf = cbuf
        self.dtr = dtr
        self.y = y


def _mixer_kernel(xa_ref, xc_ref, moda_ref, modc_ref, ph_ref, ch_ref, h0_ref,
                  nw_ref, wu_ref, wz_ref, wxbc_ref, wdt_ref,
                  pw_ref, pb_ref, ps_ref, cw_ref, cb_ref,
                  dtb_ref, alog_ref, dskip_ref, snw_ref, wout_ref,
                  x1_ref, npool_ref, nconv_ref, nssm_ref,
                  u0s, z0s, c0s, d0s, y0s, u1s, z1s, c1s, d1s, y1s, st_ref, hb_ref,
                  *, tl, q, pos0, n_steps, n_tiles):
    i = pl.program_id(0)
    b_step = lax.rem(i - 1, n_steps)
    stages = (_Stage(u0s, z0s, c0s, d0s, y0s), _Stage(u1s, z1s, c1s, d1s, y1s))

    @pl.when(i == 0)
    def _zero_first_stage():
        for ref in (u1s, z1s, c1s, d1s, y1s, st_ref):
            ref[...] = jnp.zeros(ref.shape, ref.dtype)

    def phase_a(sa):
        blk = PROJ_COL_BLOCK
        cost = tl * blk // 256
        for c0 in range(0, POOL_WIDTH, blk):
            sa.ubuf[POOL_HIST_ROWS:POOL_HIST_ROWS + tl, c0:c0 + blk] = \
                _dot(hb_ref[...], wu_ref[:, c0:c0 + blk])
            yield cost
        for c0 in range(0, CONV_DIM, blk):
            sa.cbuf[CONV_HIST_ROWS:CONV_HIST_ROWS + tl, c0:c0 + blk] = \
                _dot(hb_ref[...], wxbc_ref[:, c0:c0 + blk])
            yield cost
        sa.dtr[...] = _dot(hb_ref[...], wdt_ref[...])
        for c0 in range(0, SSD_WIDTH, blk):
            sa.z[:, c0:c0 + blk] = _dot(hb_ref[...], wz_ref[:, c0:c0 + blk])
            yield cost

    def phase_c(sc):
        blk = PROJ_COL_BLOCK
        g1 = modc_ref[0][2:3]
        for c0 in range(0, D_MODEL, blk):
            mix = _dot(sc.y[...], wout_ref[:, c0:c0 + blk])
            x1_ref[0, :, c0:c0 + blk] = xc_ref[0, :, c0:c0 + blk] + g1[:, c0:c0 + blk] * mix
            yield 2 * tl * blk // 256

    def side_pieces(sa, sc):
        a_pieces, c_pieces = phase_a(sa), phase_c(sc)
        for k, cost in enumerate(a_pieces):
            yield cost
            if k % 4 == 2:
                c_cost = next(c_pieces, None)
                if c_cost is not None:
                    yield c_cost
        yield from c_pieces

    def phase_b(sb, sa):
        ts = tl / 256.0
        yield WORK_NORM * ts

        neg_a = -jnp.exp(alog_ref[...])
        dtr = sb.dtr[...]
        decay_terms = [_ssd_decay_terms(dtr[c * q:(c + 1) * q, :], dtb_ref[...], neg_a)
                       for c in range(tl // q)]

        pos = (lax.broadcasted_iota(jnp.int32, (tl, POOL_GROUP_DIM), 0)
               + (pos0 + b_step * tl)).astype(_F32)
        ya_parts = []
        for g, w in enumerate(POOL_WINDOWS):
            lo = g * POOL_GROUP_DIM
            ug = sb.ubuf[:, lo:lo + POOL_GROUP_DIM]
            wsum = ug
            k = 1
            while k < w:
                wsum = wsum + _shift_rows(wsum, k)
                k *= 2
            cnt = jnp.minimum(jnp.float32(w), pos + 1.0)
            pooled = wsum[POOL_HIST_ROWS:, :] / cnt - ug[POOL_HIST_ROWS:, :]
            ya_parts.append(_dot(pooled.astype(_BF16), pw_ref[g]))
            yield WORK_POOL_GROUP[g] * ts
        ya = (jnp.concatenate(ya_parts, axis=1) + pb_ref[...]) * ps_ref[...]

        act_parts = []
        for c0 in range(0, CONV_DIM, PROJ_COL_BLOCK):
            cc = slice(c0, c0 + PROJ_COL_BLOCK)
            full = sb.cbuf[:, cc]
            prev = _shift_rows(full, 1)
            near = full * cw_ref[3:4, cc] + prev * cw_ref[2:3, cc]
            far = full * cw_ref[1:2, cc] + prev * cw_ref[0:1, cc]
            conv = cb_ref[:, cc] + near + _shift_rows(far, 2)
            act_parts.append(_silu(conv[CONV_HIST_ROWS:, :]))
            yield WORK_CONV_BLOCK * ts
        xbc_act = jnp.concatenate(act_parts, axis=1)
        b_off = SSD_WIDTH
        c_off = SSD_WIDTH + SSD_GROUPS * SSD_STATE

        y_parts = []
        for c in range(tl // q):
            r = slice(c * q, (c + 1) * q)
            yield from _ssd_chunk(xbc_act[r, 0:SSD_WIDTH], xbc_act[r, b_off:c_off],
                                  xbc_act[r, c_off:CONV_DIM], decay_terms[c], st_ref,
                                  dskip_ref[...], y_parts)
        y = jnp.concatenate(y_parts, axis=0) if len(y_parts) > 1 else y_parts[0]

        y = y * _silu(sb.z[...])
        yb = jnp.concatenate(
            [_rms(y[:, g * SSD_GROUP_WIDTH:(g + 1) * SSD_GROUP_WIDTH]) for g in range(SSD_GROUPS)],
            axis=1) * snw_ref[...]
        sa.y[:, 0:POOL_WIDTH] = ya.astype(_BF16)
        sa.y[:, POOL_WIDTH:POOL_WIDTH + SSD_WIDTH] = yb.astype(_BF16)

        sa.ubuf[0:POOL_HIST_ROWS, :] = sb.ubuf[tl:tl + POOL_HIST_ROWS, :]
        sa.cbuf[0:CONV_HIST_ROWS, :] = sb.cbuf[tl:tl + CONV_HIST_ROWS, :]
        yield WORK_GATE * ts

    for parity in (0, 1):
        @pl.when(lax.rem(i, 2) == parity)
        def _step(parity=parity):
            sa, sb = stages[parity], stages[1 - parity]

            @pl.when(b_step == 0)
            def _load_history():
                sb.ubuf[0:POOL_HIST_ROWS, :] = ph_ref[0]
                sb.cbuf[0:CONV_HIST_ROWS, :] = ch_ref[0]
                st_ref[...] = h0_ref[0].reshape(SSD_WIDTH, SSD_STATE).T

            x = xa_ref[0]
            mod = moda_ref[0]
            sh1, sc1 = mod[0:1], mod[1:2]
            h = _rms(x) * (nw_ref[...] * (1.0 + sc1)) + sh1
            hb_ref[...] = h.astype(_BF16)

            _interleave(phase_b(sb, sa), side_pieces(sa, sb), SIDE_PER_MAIN)

            @pl.when((b_step == n_steps - 1) & (i <= n_tiles))
            def _emit_caches():
                npool_ref[0] = sb.ubuf[tl:tl + POOL_HIST_ROWS, :]
                nconv_ref[0] = sb.cbuf[tl:tl + CONV_HIST_ROWS, :]
                nssm_ref[0] = st_ref[...].T.reshape(SSD_HEADS, SSD_HEAD_DIM, SSD_STATE)


def _mixer(x, mod, pool_hist, conv_hist, h0, wts, *, pos0, tl):
    nb, seq, _ = x.shape
    assert seq % tl == 0
    n_steps = seq // tl
    n_tiles = nb * n_steps
    q = min(tl, SSD_CHUNK)
    assert tl % q == 0
    ph = jnp.pad(pool_hist, ((0, 0), (POOL_HIST_ROWS - POOL_HIST, 0), (0, 0)))
    ch = jnp.pad(conv_hist, ((0, 0), (CONV_HIST_ROWS - CONV_HIST, 0), (0, 0)))

    def const(shape):
        return pl.BlockSpec(shape, lambda i: (0,) * len(shape))

    def tile_a(i):
        return jnp.minimum(i, n_tiles - 1)

    def tile_b(i):
        return jnp.clip(i - 1, 0, n_tiles - 1)

    def tile_c(i):
        return jnp.maximum(i - 2, 0)

    in_specs = [
        pl.BlockSpec((1, tl, D_MODEL), lambda i: (tile_a(i) // n_steps, tile_a(i) % n_steps, 0)),
        pl.BlockSpec((1, tl, D_MODEL), lambda i: (tile_c(i) // n_steps, tile_c(i) % n_steps, 0)),
        pl.BlockSpec((1, 6, D_MODEL), lambda i: (tile_a(i) // n_steps, 0, 0)),
        pl.BlockSpec((1, 6, D_MODEL), lambda i: (tile_c(i) // n_steps, 0, 0)),
        pl.BlockSpec((1, POOL_HIST_ROWS, POOL_WIDTH), lambda i: (tile_b(i) // n_steps, 0, 0)),
        pl.BlockSpec((1, CONV_HIST_ROWS, CONV_DIM), lambda i: (tile_b(i) // n_steps, 0, 0)),
        pl.BlockSpec((1, SSD_HEADS, SSD_HEAD_DIM, SSD_STATE), lambda i: (tile_b(i) // n_steps, 0, 0, 0)),
        const((1, D_MODEL)),
        const((D_MODEL, POOL_WIDTH)),
        const((D_MODEL, SSD_WIDTH)),
        const((D_MODEL, CONV_DIM)),
        const((D_MODEL, SSD_HEADS)),
        const((POOL_GROUPS, POOL_GROUP_DIM, POOL_GROUP_DIM)),
        const((1, POOL_WIDTH)),
        const((1, POOL_WIDTH)),
        const((SSD_CONV, CONV_DIM)),
        const((1, CONV_DIM)),
        const((1, SSD_HEADS)),
        const((1, SSD_HEADS)),
        const((1, SSD_WIDTH)),
        const((1, SSD_WIDTH)),
        const((POOL_WIDTH + SSD_WIDTH, D_MODEL)),
    ]
    out_specs = [
        pl.BlockSpec((1, tl, D_MODEL), lambda i: (tile_c(i) // n_steps, tile_c(i) % n_steps, 0)),
        pl.BlockSpec((1, POOL_HIST_ROWS, POOL_WIDTH), lambda i: (tile_b(i) // n_steps, 0, 0)),
        pl.BlockSpec((1, CONV_HIST_ROWS, CONV_DIM), lambda i: (tile_b(i) // n_steps, 0, 0)),
        pl.BlockSpec((1, SSD_HEADS, SSD_HEAD_DIM, SSD_STATE), lambda i: (tile_b(i) // n_steps, 0, 0, 0)),
    ]
    out_shape = [
        jax.ShapeDtypeStruct((nb, seq, D_MODEL), _F32),
        jax.ShapeDtypeStruct((nb, POOL_HIST_ROWS, POOL_WIDTH), _F32),
        jax.ShapeDtypeStruct((nb, CONV_HIST_ROWS, CONV_DIM), _F32),
        jax.ShapeDtypeStruct((nb, SSD_HEADS, SSD_HEAD_DIM, SSD_STATE), _F32),
    ]
    x1, npool, nconv, nssm = pl.pallas_call(
        functools.partial(_mixer_kernel, tl=tl, q=q, pos0=pos0, n_steps=n_steps, n_tiles=n_tiles),
        grid=(n_tiles + 2,),
        in_specs=in_specs,
        out_specs=out_specs,
        out_shape=out_shape,
        scratch_shapes=2 * [
            pltpu.VMEM((POOL_HIST_ROWS + tl, POOL_WIDTH), _F32),
            pltpu.VMEM((tl, SSD_WIDTH), _F32),
            pltpu.VMEM((CONV_HIST_ROWS + tl, CONV_DIM), _F32),
            pltpu.VMEM((tl, SSD_HEADS), _F32),
            pltpu.VMEM((tl, POOL_WIDTH + SSD_WIDTH), _BF16),
        ] + [pltpu.VMEM((SSD_STATE, SSD_WIDTH), _F32), pltpu.VMEM((tl, D_MODEL), _BF16)],
        compiler_params=pltpu.CompilerParams(
            dimension_semantics=("arbitrary",),
            vmem_limit_bytes=V7X_VMEM_LIMIT_BYTES),
        name="mixer",
    )(x, x, mod, mod, ph, ch, h0, *wts)
    return x1, npool[:, POOL_HIST_ROWS - POOL_HIST:], nconv[:, CONV_HIST_ROWS - CONV_HIST:], nssm


def _ffn_kernel(x_ref, mod_ref, nw_ref, w1_ref, b1_ref, w2_ref, b2_ref, fw_ref, y_ref, *, ff_chunk):
    tb, tl, _ = x_ref.shape
    x3 = x_ref[...]
    mod = mod_ref[...]
    sh2, sc2, g2 = mod[:, 3:4, :], mod[:, 4:5, :], mod[:, 5:6, :]
    h3 = (_rms(x3) * nw_ref[...]) * (1.0 + sc2) + sh2
    hb = h3.reshape(tb * tl, D_MODEL).astype(_BF16)
    acc = None
    for j in range(D_FF // ff_chunk):
        cols = slice(j * ff_chunk, (j + 1) * ff_chunk)
        f = jnp.maximum(_dot(hb, w1_ref[:, cols]) + b1_ref[:, cols], 0.0)
        part = _dot((f * f).astype(_BF16), w2_ref[cols, :])
        acc = part if acc is None else acc + part
    f2 = (acc + b2_ref[...]).reshape(tb, tl, D_MODEL)
    x2 = x3 + g2 * f2
    y_ref[...] = _rms(x2) * fw_ref[...]


def _ffn(x1, mod, wts, *, tb, tl):
    nb, seq, _ = x1.shape
    assert nb % tb == 0 and seq % tl == 0

    def const(shape):
        return pl.BlockSpec(shape, lambda b, s: (0,) * len(shape))

    return pl.pallas_call(
        functools.partial(_ffn_kernel, ff_chunk=1024),
        grid=(nb // tb, seq // tl),
        in_specs=[
            pl.BlockSpec((tb, tl, D_MODEL), lambda b, s: (b, s, 0)),
            pl.BlockSpec((tb, 6, D_MODEL), lambda b, s: (b, 0, 0)),
            const((1, D_MODEL)),
            const((D_MODEL, D_FF)),
            const((1, D_FF)),
            const((D_FF, D_MODEL)),
            const((1, D_MODEL)),
            const((1, D_MODEL)),
        ],
        out_specs=pl.BlockSpec((tb, tl, D_MODEL), lambda b, s: (b, s, 0)),
        out_shape=jax.ShapeDtypeStruct((nb, seq, D_MODEL), _F32),
        compiler_params=pltpu.CompilerParams(
            dimension_semantics=("arbitrary", "arbitrary"),
            vmem_limit_bytes=V7X_VMEM_LIMIT_BYTES),
        name="ffn",
    )(x1, mod, *wts)


def _layer_weights(l, norm_mix_w, norm_ffn_w, w_in, pool_w, pool_b, pool_scale, conv_w, conv_b,
                   dt_bias, a_log, d_skip, ssd_norm_w, w_out, w_ff1, b_ff1, w_ff2, b_ff2):
    win = w_in[l]
    o_z = POOL_WIDTH
    o_xbc = POOL_WIDTH + SSD_WIDTH
    o_dt = o_xbc + CONV_DIM
    mixer_w = (
        norm_mix_w[l].reshape(1, D_MODEL),
        win[:, :o_z].astype(_BF16),
        win[:, o_z:o_xbc].astype(_BF16),
        win[:, o_xbc:o_dt].astype(_BF16),
        win[:, o_dt:o_dt + SSD_HEADS].astype(_BF16),
        pool_w[l].astype(_BF16),
        pool_b[l].reshape(1, POOL_WIDTH),
        pool_scale[l].reshape(1, POOL_WIDTH),
        conv_w[l],
        conv_b[l].reshape(1, CONV_DIM),
        dt_bias[l].reshape(1, SSD_HEADS),
        a_log[l].reshape(1, SSD_HEADS),
        jnp.repeat(d_skip[l], SSD_HEAD_DIM).reshape(1, SSD_WIDTH),
        ssd_norm_w[l].reshape(1, SSD_WIDTH),
        w_out[l].astype(_BF16),
    )
    ffn_w = (
        norm_ffn_w[l].reshape(1, D_MODEL),
        w_ff1[l].astype(_BF16),
        b_ff1[l].reshape(1, D_FF),
        w_ff2[l].astype(_BF16),
        b_ff2[l].reshape(1, D_MODEL),
    )
    return mixer_w, ffn_w


def _run(x_prompt, x_sample, cache_pool, cache_conv, state_ssm, c_prompt, c_sample,
         norm_mix_w, norm_ffn_w, w_ada, b_ada, w_in, pool_w, pool_b, pool_scale,
         conv_w, conv_b, dt_bias, a_log, d_skip, ssd_norm_w, w_out,
         w_ff1, b_ff1, w_ff2, b_ff2, final_norm_w, *, prompt_tl, ffn_tl, past_len):
    depth = w_in.shape[0]
    assert depth == 1, "the final norm is fused into the (single) layer's ffn call"
    nbp, seq_p, _ = x_prompt.shape
    nbs, seq_s, _ = x_sample.shape
    fw = final_norm_w.reshape(1, D_MODEL)
    c_all = jnp.concatenate([c_prompt, c_sample], axis=0)
    xp, xs = x_prompt, x_sample
    outs = {k: [] for k in ("pp", "cp", "sp", "ps", "cs", "ss")}
    for l in range(depth):
        mod = _ada(c_all, w_ada[l], b_ada[l]).reshape(nbp + nbs, 6, D_MODEL)
        mod_p, mod_s = mod[:nbp], mod[nbp:]
        mixer_w, ffn_w = _layer_weights(l, norm_mix_w, norm_ffn_w, w_in, pool_w, pool_b, pool_scale,
                                        conv_w, conv_b, dt_bias, a_log, d_skip, ssd_norm_w, w_out,
                                        w_ff1, b_ff1, w_ff2, b_ff2)
        zp = jnp.zeros((nbp, POOL_HIST, POOL_WIDTH), _F32)
        zc = jnp.zeros((nbp, CONV_HIST, CONV_DIM), _F32)
        zh = jnp.zeros((nbp, SSD_HEADS, SSD_HEAD_DIM, SSD_STATE), _F32)
        xp, npool, nconv, nssm = _mixer(xp, mod_p, zp, zc, zh, mixer_w, pos0=0, tl=prompt_tl)
        xp = _ffn(xp, mod_p, ffn_w + (fw,), tb=1, tl=ffn_tl)
        outs["pp"].append(npool); outs["cp"].append(nconv); outs["sp"].append(nssm)
        xs, npool, nconv, nssm = _mixer(xs, mod_s, cache_pool[l], cache_conv[l], state_ssm[l], mixer_w,
                                        pos0=past_len, tl=seq_s)
        xs = _ffn(xs, mod_s, ffn_w + (fw,), tb=nbs, tl=seq_s)
        outs["ps"].append(npool); outs["cs"].append(nconv); outs["ss"].append(nssm)
    st = lambda k: outs[k][0][None]
    return (xp, xs, st("pp"), st("cp"), st("sp"), st("ps"), st("cs"), st("ss"))


def kernel(x_prompt, x_sample, cache_pool, cache_conv, state_ssm, c_prompt, c_sample, norm_mix_w, norm_ffn_w, w_ada, b_ada, w_in, pool_w, pool_b, pool_scale, conv_w, conv_b, dt_bias, a_log, d_skip, ssd_norm_w, w_out, w_ff1, b_ff1, w_ff2, b_ff2, final_norm_w):
    return _run(x_prompt, x_sample, cache_pool, cache_conv, state_ssm, c_prompt, c_sample,
                norm_mix_w, norm_ffn_w, w_ada, b_ada, w_in, pool_w, pool_b, pool_scale,
                conv_w, conv_b, dt_bias, a_log, d_skip, ssd_norm_w, w_out,
                w_ff1, b_ff1, w_ff2, b_ff2, final_norm_w,
                prompt_tl=512, ffn_tl=1024, past_len=4096)
```

```python
import functools

import jax
import jax.numpy as jnp
from jax import lax
from jax.experimental import pallas as pl
from jax.experimental.pallas import tpu as pltpu

D_MODEL = 1024
POOL_WIDTH = 1024
POOL_GROUPS = 4
POOL_GROUP_DIM = POOL_WIDTH // POOL_GROUPS
POOL_WINDOWS = (2, 4, 8, 16)
POOL_HIST = 15
POOL_HIST_ROWS = 16
SSD_WIDTH = 1024
SSD_HEAD_DIM = 64
SSD_HEADS = 16
SSD_GROUPS = 2
SSD_GROUP_WIDTH = SSD_WIDTH // SSD_GROUPS
SSD_STATE = 128
SSD_CONV = 4
CONV_HIST = SSD_CONV - 1
CONV_HIST_ROWS = 8
CONV_DIM = SSD_WIDTH + 2 * SSD_GROUPS * SSD_STATE
D_IN_PROJ = POOL_WIDTH + SSD_WIDTH + CONV_DIM + SSD_HEADS
D_FF = 4 * D_MODEL
EPS = 1e-6

V7X_VMEM_LIMIT_BYTES = 56 * 1024 * 1024
LANES = 128
LOG2_E = 1.4426950408889634
SSD_CHUNK = 128
PROJ_COL_BLOCK = 256
WORK_NORM = 0
WORK_POOL_GROUP = (130, 270, 400, 400)
WORK_CONV_BLOCK = 220
WORK_SSD_PRELUDE = 300
WORK_SSD_HEAD_PAIR = 130
WORK_SSD_TAIL = 350
WORK_GATE = 400
SIDE_PER_MAIN = 0.95

_BF16 = jnp.bfloat16
_F32 = jnp.float32


def _dot(a, b):
    return jnp.dot(a, b, preferred_element_type=_F32)


def _dot_nt(a, b):
    return lax.dot_general(a, b, (((1,), (1,)), ((), ())), preferred_element_type=_F32)


def _dot_tn(a, b):
    return lax.dot_general(a, b, (((0,), (0,)), ((), ())), preferred_element_type=_F32)


def _split3(v):
    v1 = v.astype(_BF16)
    r1 = v - v1.astype(_F32)
    v2 = r1.astype(_BF16)
    v3 = (r1 - v2.astype(_F32)).astype(_BF16)
    return v1, v2, v3


def _split2(v):
    v1 = v.astype(_BF16)
    v2 = (v - v1.astype(_F32)).astype(_BF16)
    return v1, v2


def _silu(v):
    hv = 0.5 * v
    return hv + hv * jnp.tanh(hv)


def _softplus(v):
    return jnp.maximum(v, 0.0) + jnp.log1p(jnp.exp(-jnp.abs(v)))


def _rms(v):
    return v * lax.rsqrt(jnp.mean(v * v, axis=-1, keepdims=True) + EPS)


def _shift_rows(v, k):
    return pltpu.roll(v, k, axis=0)


def _ada_kernel(c_ref, w_ref, b_ref, o_ref):
    s = _silu(c_ref[...]).astype(_BF16)
    o_ref[...] = _dot(s, w_ref[...].astype(_BF16)) + b_ref[...]


def _ada(c_all, w_ada, b_ada):
    nb = c_all.shape[0]
    n_out = w_ada.shape[1]
    blk = D_MODEL
    return pl.pallas_call(
        _ada_kernel,
        grid=(n_out // blk,),
        in_specs=[
            pl.BlockSpec((nb, D_MODEL), lambda j: (0, 0)),
            pl.BlockSpec((D_MODEL, blk), lambda j: (0, j)),
            pl.BlockSpec((1, blk), lambda j: (0, j)),
        ],
        out_specs=pl.BlockSpec((nb, blk), lambda j: (0, j)),
        out_shape=jax.ShapeDtypeStruct((nb, n_out), _F32),
        compiler_params=pltpu.CompilerParams(dimension_semantics=("arbitrary",)),
        name="ada",
    )(c_all, w_ada, b_ada.reshape(1, n_out))


def _interleave(main, side, side_per_main):
    main_done = 0.0
    side_done = 0.0
    side_left = True
    for work in main:
        main_done += work
        while side_left and side_done < main_done * side_per_main:
            cost = next(side, None)
            side_left = cost is not None
            side_done += cost or 0.0
    for _ in side:
        pass


def _ssd_decay_terms(dtr, dtb, neg_a):
    q = dtr.shape[0]
    dt_c = _softplus(dtr + dtb)
    a_c = dt_c * neg_a
    ri = lax.broadcasted_iota(jnp.int32, (q, q), 0)
    ci = lax.broadcasted_iota(jnp.int32, (q, q), 1)
    tri = jnp.where(ri >= ci, 1.0, 0.0).astype(_BF16)
    a1, a2, a3 = _split3(a_c)
    acs_c = _dot(jnp.concatenate([tri, tri, tri], axis=1),
                 jnp.concatenate([a1, a2, a3], axis=0))

    side = max(q, LANES)
    acs2_c = acs_c * LOG2_E
    cols = jnp.concatenate([dt_c, acs2_c, jnp.zeros((q, LANES - 2 * SSD_HEADS), _F32)], axis=1)
    if q < side:
        cols = jnp.concatenate([cols, jnp.zeros((side - q, LANES), _F32)], axis=0)
    rows = cols.T
    dt_r = rows[0:SSD_HEADS, 0:q]
    acs2_r = rows[SSD_HEADS:2 * SSD_HEADS, 0:q]
    eacs = jnp.exp(acs_c)
    w_state = dt_c * jnp.exp(acs_c[q - 1:q, :] - acs_c)
    return dt_r, acs2_c, acs2_r, eacs, w_state


def _ssd_chunk(xs, bm, cm, decay_terms, st_ref, dskip, out):
    q = xs.shape[0]
    dt_r, acs2_c, acs2_r, eacs, w_state = decay_terms

    hi = lax.broadcasted_iota(jnp.int32, (2 * SSD_HEADS, SSD_WIDTH), 0) % SSD_HEADS
    chn = lax.broadcasted_iota(jnp.int32, (2 * SSD_HEADS, SSD_WIDTH), 1)
    expand2 = jnp.where((chn >= hi * SSD_HEAD_DIM) & (chn < (hi + 1) * SSD_HEAD_DIM),
                        1.0, 0.0).astype(_BF16)

    def _expand_heads(v):
        return _dot(jnp.concatenate(_split2(v), axis=1), expand2)

    eacs_rep = _expand_heads(eacs)
    w_state_rep = _expand_heads(w_state)
    chunk_decay_rep = eacs_rep[q - 1:q, :]
    ri = lax.broadcasted_iota(jnp.int32, (q, q), 0)
    ci = lax.broadcasted_iota(jnp.int32, (q, q), 1)
    causal = ri >= ci

    lane = lax.broadcasted_iota(jnp.int32, (q, 2 * SSD_HEAD_DIM), 1)
    first_head = lane < SSD_HEAD_DIM
    bgs = [bm[:, g * SSD_STATE:(g + 1) * SSD_STATE].astype(_BF16) for g in range(SSD_GROUPS)]
    cgs = [cm[:, g * SSD_STATE:(g + 1) * SSD_STATE].astype(_BF16) for g in range(SSD_GROUPS)]
    cbm = [jnp.where(causal, _dot_nt(cgs[g], bgs[g]), 0.0) for g in range(SSD_GROUPS)]
    yield WORK_SSD_PRELUDE
    yd_parts = []
    for p in range(SSD_HEADS // 2):
        g = (2 * p) // (SSD_HEADS // SSD_GROUPS)
        ms = []
        for hh in (2 * p, 2 * p + 1):
            seg2 = jnp.minimum(acs2_c[:, hh:hh + 1] - acs2_r[hh:hh + 1, :], 0.0)
            ms.append((cbm[g] * jnp.exp2(seg2) * dt_r[hh:hh + 1, :]).astype(_BF16))
        xp = xs[:, p * 2 * SSD_HEAD_DIM:(p + 1) * 2 * SSD_HEAD_DIM]
        rhs = jnp.concatenate([jnp.where(first_head, xp, 0.0), jnp.where(first_head, 0.0, xp)],
                              axis=0).astype(_BF16)
        yd_parts.append(_dot(jnp.concatenate(ms, axis=1), rhs))
        yield WORK_SSD_HEAD_PAIR
    y_diag = jnp.concatenate(yd_parts, axis=1)

    xw = (xs * w_state_rep).astype(_BF16)
    yo_parts = []
    for g in range(SSD_GROUPS):
        gc = slice(g * SSD_GROUP_WIDTH, (g + 1) * SSD_GROUP_WIDTH)
        st_g = st_ref[:, gc]
        yo_parts.append(_dot(cgs[g], st_g.astype(_BF16)))
        st_ref[:, gc] = st_g * chunk_decay_rep[:, gc] + _dot_tn(bgs[g], xw[:, gc])
    y_off = jnp.concatenate(yo_parts, axis=1) * eacs_rep
    out.append(y_diag + y_off + dskip * xs)
    yield WORK_SSD_TAIL


class _Stage:
    def __init__(self, ubuf, z, cbuf, dtr, y):
        self.ubuf = ubuf
        self.z = z
        self.cbuf = cbuf
        self.dtr = dtr
        self.y = y


def _mixer_kernel(xa_ref, xc_ref, moda_ref, modc_ref, ph_ref, ch_ref, h0_ref,
                  nw_ref, win_ref,
                  pw_ref, pb_ref, ps_ref, cw_ref, cb_ref,
                  dtb_ref, alog_ref, dskip_ref, snw_ref, wout_ref,
                  x1_ref, npool_ref, nconv_ref, nssm_ref,
                  u0s, z0s, c0s, d0s, y0s, u1s, z1s, c1s, d1s, y1s, st_ref, hb_ref,
                  *, tl, q, pos0, n_steps, n_tiles):
    i = pl.program_id(0)
    b_step = lax.rem(i - 1, n_steps)
    stages = (_Stage(u0s, z0s, c0s, d0s, y0s), _Stage(u1s, z1s, c1s, d1s, y1s))

    @pl.when(i == 0)
    def _zero_first_stage():
        for ref in (u1s, z1s, c1s, d1s, y1s, st_ref):
            ref[...] = jnp.zeros(ref.shape, ref.dtype)

    def phase_a(sa):
        blk = PROJ_COL_BLOCK
        cost = tl * blk // 256
        o_z, o_xbc, o_dt = POOL_WIDTH, POOL_WIDTH + SSD_WIDTH, POOL_WIDTH + SSD_WIDTH + CONV_DIM
        for c0 in range(0, POOL_WIDTH, blk):
            sa.ubuf[POOL_HIST_ROWS:POOL_HIST_ROWS + tl, c0:c0 + blk] = \
                _dot(hb_ref[...], win_ref[:, c0:c0 + blk])
            yield cost
        for c0 in range(0, CONV_DIM, blk):
            sa.cbuf[CONV_HIST_ROWS:CONV_HIST_ROWS + tl, c0:c0 + blk] = \
                _dot(hb_ref[...], win_ref[:, o_xbc + c0:o_xbc + c0 + blk])
            yield cost
        sa.dtr[...] = _dot(hb_ref[...], win_ref[:, o_dt:o_dt + SSD_HEADS])
        for c0 in range(0, SSD_WIDTH, blk):
            sa.z[:, c0:c0 + blk] = _dot(hb_ref[...], win_ref[:, o_z + c0:o_z + c0 + blk])
            yield cost

    def phase_c(sc):
        blk = PROJ_COL_BLOCK
        g1 = modc_ref[0][2:3]
        for c0 in range(0, D_MODEL, blk):
            mix = _dot(sc.y[...], wout_ref[:, c0:c0 + blk])
            x1_ref[0, :, c0:c0 + blk] = xc_ref[0, :, c0:c0 + blk] + g1[:, c0:c0 + blk] * mix
            yield 2 * tl * blk // 256

    def side_pieces(sa, sc):
        yield from phase_c(sc)
        yield from phase_a(sa)

    def phase_b(sb, sa):
        ts = tl / 256.0
        yield WORK_NORM * ts

        neg_a = -jnp.exp(alog_ref[...])
        dtr = sb.dtr[...]
        decay_terms = [_ssd_decay_terms(dtr[c * q:(c + 1) * q, :], dtb_ref[...], neg_a)
                       for c in range(tl // q)]

        pos = (lax.broadcasted_iota(jnp.int32, (tl, POOL_GROUP_DIM), 0)
               + (pos0 + b_step * tl)).astype(_F32)
        ya_parts = []
        for g, w in enumerate(POOL_WINDOWS):
            lo = g * POOL_GROUP_DIM
            ug = sb.ubuf[:, lo:lo + POOL_GROUP_DIM]
            wsum = ug
            k = 1
            while k < w:
                wsum = wsum + _shift_rows(wsum, k)
                k *= 2
            cnt = jnp.minimum(jnp.float32(w), pos + 1.0)
            pooled = wsum[POOL_HIST_ROWS:, :] / cnt - ug[POOL_HIST_ROWS:, :]
            ya_parts.append(_dot(pooled.astype(_BF16), pw_ref[g]))
            yield WORK_POOL_GROUP[g] * ts
        ya = (jnp.concatenate(ya_parts, axis=1) + pb_ref[...]) * ps_ref[...]

        act_parts = []
        for c0 in range(0, CONV_DIM, PROJ_COL_BLOCK):
            cc = slice(c0, c0 + PROJ_COL_BLOCK)
            full = sb.cbuf[:, cc]
            prev = _shift_rows(full, 1)
            near = full * cw_ref[3:4, cc] + prev * cw_ref[2:3, cc]
            far = full * cw_ref[1:2, cc] + prev * cw_ref[0:1, cc]
            conv = cb_ref[:, cc] + near + _shift_rows(far, 2)
            act_parts.append(_silu(conv[CONV_HIST_ROWS:, :]))
            yield WORK_CONV_BLOCK * ts
        xbc_act = jnp.concatenate(act_parts, axis=1)
        b_off = SSD_WIDTH
        c_off = SSD_WIDTH + SSD_GROUPS * SSD_STATE

        y_parts = []
        for c in range(tl // q):
            r = slice(c * q, (c + 1) * q)
            yield from _ssd_chunk(xbc_act[r, 0:SSD_WIDTH], xbc_act[r, b_off:c_off],
                                  xbc_act[r, c_off:CONV_DIM], decay_terms[c], st_ref,
                                  dskip_ref[...], y_parts)
        y = jnp.concatenate(y_parts, axis=0) if len(y_parts) > 1 else y_parts[0]

        y = y * _silu(sb.z[...])
        yb = jnp.concatenate(
            [_rms(y[:, g * SSD_GROUP_WIDTH:(g + 1) * SSD_GROUP_WIDTH]) for g in range(SSD_GROUPS)],
            axis=1) * snw_ref[...]
        sa.y[:, 0:POOL_WIDTH] = ya.astype(_BF16)
        sa.y[:, POOL_WIDTH:POOL_WIDTH + SSD_WIDTH] = yb.astype(_BF16)

        sa.ubuf[0:POOL_HIST_ROWS, :] = sb.ubuf[tl:tl + POOL_HIST_ROWS, :]
        sa.cbuf[0:CONV_HIST_ROWS, :] = sb.cbuf[tl:tl + CONV_HIST_ROWS, :]
        yield WORK_GATE * ts

    for parity in (0, 1):
        @pl.when(lax.rem(i, 2) == parity)
        def _step(parity=parity):
            sa, sb = stages[parity], stages[1 - parity]

            @pl.when(b_step == 0)
            def _load_history():
                sb.ubuf[0:POOL_HIST_ROWS, :] = ph_ref[0]
                sb.cbuf[0:CONV_HIST_ROWS, :] = ch_ref[0]
                st_ref[...] = h0_ref[0].reshape(SSD_WIDTH, SSD_STATE).T

            x = xa_ref[0]
            mod = moda_ref[0]
            sh1, sc1 = mod[0:1], mod[1:2]
            h = _rms(x) * (nw_ref[...] * (1.0 + sc1)) + sh1
            hb_ref[...] = h.astype(_BF16)

            _interleave(phase_b(sb, sa), side_pieces(sa, sb), SIDE_PER_MAIN)

            @pl.when((b_step == n_steps - 1) & (i <= n_tiles))
            def _emit_caches():
                npool_ref[0] = sb.ubuf[tl:tl + POOL_HIST_ROWS, :]
                nconv_ref[0] = sb.cbuf[tl:tl + CONV_HIST_ROWS, :]
                nssm_ref[0] = st_ref[...].T.reshape(SSD_HEADS, SSD_HEAD_DIM, SSD_STATE)


def _mixer(x, mod, pool_hist, conv_hist, h0, wts, *, pos0, tl):
    nb, seq, _ = x.shape
    assert seq % tl == 0
    n_steps = seq // tl
    n_tiles = nb * n_steps
    q = min(tl, SSD_CHUNK)
    assert tl % q == 0
    ph = jnp.pad(pool_hist, ((0, 0), (POOL_HIST_ROWS - POOL_HIST, 0), (0, 0)))
    ch = jnp.pad(conv_hist, ((0, 0), (CONV_HIST_ROWS - CONV_HIST, 0), (0, 0)))

    def const(shape):
        return pl.BlockSpec(shape, lambda i: (0,) * len(shape))

    def tile_a(i):
        return jnp.minimum(i, n_tiles - 1)

    def tile_b(i):
        return jnp.clip(i - 1, 0, n_tiles - 1)

    def tile_c(i):
        return jnp.maximum(i - 2, 0)

    in_specs = [
        pl.BlockSpec((1, tl, D_MODEL), lambda i: (tile_a(i) // n_steps, tile_a(i) % n_steps, 0)),
        pl.BlockSpec((1, tl, D_MODEL), lambda i: (tile_c(i) // n_steps, tile_c(i) % n_steps, 0)),
        pl.BlockSpec((1, 6, D_MODEL), lambda i: (tile_a(i) // n_steps, 0, 0)),
        pl.BlockSpec((1, 6, D_MODEL), lambda i: (tile_c(i) // n_steps, 0, 0)),
        pl.BlockSpec((1, POOL_HIST_ROWS, POOL_WIDTH), lambda i: (tile_b(i) // n_steps, 0, 0)),
        pl.BlockSpec((1, CONV_HIST_ROWS, CONV_DIM), lambda i: (tile_b(i) // n_steps, 0, 0)),
        pl.BlockSpec((1, SSD_HEADS, SSD_HEAD_DIM, SSD_STATE), lambda i: (tile_b(i) // n_steps, 0, 0, 0)),
        const((1, D_MODEL)),
        const((D_MODEL, D_IN_PROJ)),
        const((POOL_GROUPS, POOL_GROUP_DIM, POOL_GROUP_DIM)),
        const((1, POOL_WIDTH)),
        const((1, POOL_WIDTH)),
        const((SSD_CONV, CONV_DIM)),
        const((1, CONV_DIM)),
        const((1, SSD_HEADS)),
        const((1, SSD_HEADS)),
        const((1, SSD_WIDTH)),
        const((1, SSD_WIDTH)),
        const((POOL_WIDTH + SSD_WIDTH, D_MODEL)),
    ]
    out_specs = [
        pl.BlockSpec((1, tl, D_MODEL), lambda i: (tile_c(i) // n_steps, tile_c(i) % n_steps, 0)),
        pl.BlockSpec((1, POOL_HIST_ROWS, POOL_WIDTH), lambda i: (tile_b(i) // n_steps, 0, 0)),
        pl.BlockSpec((1, CONV_HIST_ROWS, CONV_DIM), lambda i: (tile_b(i) // n_steps, 0, 0)),
        pl.BlockSpec((1, SSD_HEADS, SSD_HEAD_DIM, SSD_STATE), lambda i: (tile_b(i) // n_steps, 0, 0, 0)),
    ]
    out_shape = [
        jax.ShapeDtypeStruct((nb, seq, D_MODEL), _F32),
        jax.ShapeDtypeStruct((nb, POOL_HIST_ROWS, POOL_WIDTH), _F32),
        jax.ShapeDtypeStruct((nb, CONV_HIST_ROWS, CONV_DIM), _F32),
        jax.ShapeDtypeStruct((nb, SSD_HEADS, SSD_HEAD_DIM, SSD_STATE), _F32),
    ]
    x1, npool, nconv, nssm = pl.pallas_call(
        functools.partial(_mixer_kernel, tl=tl, q=q, pos0=pos0, n_steps=n_steps, n_tiles=n_tiles),
        grid=(n_tiles + 2,),
        in_specs=in_specs,
        out_specs=out_specs,
        out_shape=out_shape,
        scratch_shapes=2 * [
            pltpu.VMEM((POOL_HIST_ROWS + tl, POOL_WIDTH), _F32),
            pltpu.VMEM((tl, SSD_WIDTH), _F32),
            pltpu.VMEM((CONV_HIST_ROWS + tl, CONV_DIM), _F32),
            pltpu.VMEM((tl, SSD_HEADS), _F32),
            pltpu.VMEM((tl, POOL_WIDTH + SSD_WIDTH), _BF16),
        ] + [pltpu.VMEM((SSD_STATE, SSD_WIDTH), _F32), pltpu.VMEM((tl, D_MODEL), _BF16)],
        compiler_params=pltpu.CompilerParams(
            dimension_semantics=("arbitrary",),
            vmem_limit_bytes=V7X_VMEM_LIMIT_BYTES),
        name="mixer",
    )(x, x, mod, mod, ph, ch, h0, *wts)
    return x1, npool[:, POOL_HIST_ROWS - POOL_HIST:], nconv[:, CONV_HIST_ROWS - CONV_HIST:], nssm


def _ffn_kernel(x_ref, mod_ref, nw_ref, w1_ref, b1_ref, w2_ref, b2_ref, fw_ref, y_ref, *, ff_chunk):
    tb, tl, _ = x_ref.shape
    x3 = x_ref[...]
    mod = mod_ref[...]
    sh2, sc2, g2 = mod[:, 3:4, :], mod[:, 4:5, :], mod[:, 5:6, :]
    h3 = (_rms(x3) * nw_ref[...]) * (1.0 + sc2) + sh2
    hb = h3.reshape(tb * tl, D_MODEL).astype(_BF16)
    acc = None
    for j in range(D_FF // ff_chunk):
        cols = slice(j * ff_chunk, (j + 1) * ff_chunk)
        f = jnp.maximum(_dot(hb, w1_ref[:, cols]) + b1_ref[:, cols], 0.0)
        part = _dot((f * f).astype(_BF16), w2_ref[cols, :])
        acc = part if acc is None else acc + part
    f2 = (acc + b2_ref[...]).reshape(tb, tl, D_MODEL)
    x2 = x3 + g2 * f2
    y_ref[...] = _rms(x2) * fw_ref[...]


def _ffn(x1, mod, wts, *, tb, tl):
    nb, seq, _ = x1.shape
    assert nb % tb == 0 and seq % tl == 0

    def const(shape):
        return pl.BlockSpec(shape, lambda b, s: (0,) * len(shape))

    return pl.pallas_call(
        functools.partial(_ffn_kernel, ff_chunk=1024),
        grid=(nb // tb, seq // tl),
        in_specs=[
            pl.BlockSpec((tb, tl, D_MODEL), lambda b, s: (b, s, 0)),
            pl.BlockSpec((tb, 6, D_MODEL), lambda b, s: (b, 0, 0)),
            const((1, D_MODEL)),
            const((D_MODEL, D_FF)),
            const((1, D_FF)),
            const((D_FF, D_MODEL)),
            const((1, D_MODEL)),
            const((1, D_MODEL)),
        ],
        out_specs=pl.BlockSpec((tb, tl, D_MODEL), lambda b, s: (b, s, 0)),
        out_shape=jax.ShapeDtypeStruct((nb, seq, D_MODEL), _F32),
        compiler_params=pltpu.CompilerParams(
            dimension_semantics=("arbitrary", "arbitrary"),
            vmem_limit_bytes=V7X_VMEM_LIMIT_BYTES),
        name="ffn",
    )(x1, mod, *wts)


def _layer_weights(l, norm_mix_w, norm_ffn_w, w_in, pool_w, pool_b, pool_scale, conv_w, conv_b,
                   dt_bias, a_log, d_skip, ssd_norm_w, w_out, w_ff1, b_ff1, w_ff2, b_ff2):
    mixer_w = (
        norm_mix_w[l].reshape(1, D_MODEL),
        w_in[l].astype(_BF16),
        pool_w[l].astype(_BF16),
        pool_b[l].reshape(1, POOL_WIDTH),
        pool_scale[l].reshape(1, POOL_WIDTH),
        conv_w[l],
        conv_b[l].reshape(1, CONV_DIM),
        dt_bias[l].reshape(1, SSD_HEADS),
        a_log[l].reshape(1, SSD_HEADS),
        jnp.repeat(d_skip[l], SSD_HEAD_DIM).reshape(1, SSD_WIDTH),
        ssd_norm_w[l].reshape(1, SSD_WIDTH),
        w_out[l].astype(_BF16),
    )
    ffn_w = (
        norm_ffn_w[l].reshape(1, D_MODEL),
        w_ff1[l].astype(_BF16),
        b_ff1[l].reshape(1, D_FF),
        w_ff2[l].astype(_BF16),
        b_ff2[l].reshape(1, D_MODEL),
    )
    return mixer_w, ffn_w


def _run(x_prompt, x_sample, cache_pool, cache_conv, state_ssm, c_prompt, c_sample,
         norm_mix_w, norm_ffn_w, w_ada, b_ada, w_in, pool_w, pool_b, pool_scale,
         conv_w, conv_b, dt_bias, a_log, d_skip, ssd_norm_w, w_out,
         w_ff1, b_ff1, w_ff2, b_ff2, final_norm_w, *, prompt_tl, ffn_tl, past_len):
    depth = w_in.shape[0]
    assert depth == 1, "the final norm is fused into the (single) layer's ffn call"
    nbp, seq_p, _ = x_prompt.shape
    nbs, seq_s, _ = x_sample.shape
    fw = final_norm_w.reshape(1, D_MODEL)
    c_all = jnp.concatenate([c_prompt, c_sample], axis=0)
    xp, xs = x_prompt, x_sample
    outs = {k: [] for k in ("pp", "cp", "sp", "ps", "cs", "ss")}
    for l in range(depth):
        mod = _ada(c_all, w_ada[l], b_ada[l]).reshape(nbp + nbs, 6, D_MODEL)
        mod_p, mod_s = mod[:nbp], mod[nbp:]
        mixer_w, ffn_w = _layer_weights(l, norm_mix_w, norm_ffn_w, w_in, pool_w, pool_b, pool_scale,
                                        conv_w, conv_b, dt_bias, a_log, d_skip, ssd_norm_w, w_out,
                                        w_ff1, b_ff1, w_ff2, b_ff2)
        zp = jnp.zeros((nbp, POOL_HIST, POOL_WIDTH), _F32)
        zc = jnp.zeros((nbp, CONV_HIST, CONV_DIM), _F32)
        zh = jnp.zeros((nbp, SSD_HEADS, SSD_HEAD_DIM, SSD_STATE), _F32)
        xp, npool, nconv, nssm = _mixer(xp, mod_p, zp, zc, zh, mixer_w, pos0=0, tl=prompt_tl)
        xp = _ffn(xp, mod_p, ffn_w + (fw,), tb=1, tl=ffn_tl)
        outs["pp"].append(npool); outs["cp"].append(nconv); outs["sp"].append(nssm)
        xs, npool, nconv, nssm = _mixer(xs, mod_s, cache_pool[l], cache_conv[l], state_ssm[l], mixer_w,
                                        pos0=past_len, tl=seq_s)
        xs = _ffn(xs, mod_s, ffn_w + (fw,), tb=nbs, tl=seq_s)
        outs["ps"].append(npool); outs["cs"].append(nconv); outs["ss"].append(nssm)
    st = lambda k: outs[k][0][None]
    return (xp, xs, st("pp"), st("cp"), st("sp"), st("ps"), st("cs"), st("ss"))


def kernel(x_prompt, x_sample, cache_pool, cache_conv, state_ssm, c_prompt, c_sample, norm_mix_w, norm_ffn_w, w_ada, b_ada, w_in, pool_w, pool_b, pool_scale, conv_w, conv_b, dt_bias, a_log, d_skip, ssd_norm_w, w_out, w_ff1, b_ff1, w_ff2, b_ff2, final_norm_w):
    return _run(x_prompt, x_sample, cache_pool, cache_conv, state_ssm, c_prompt, c_sample,
                norm_mix_w, norm_ffn_w, w_ada, b_ada, w_in, pool_w, pool_b, pool_scale,
                conv_w, conv_b, dt_bias, a_log, d_skip, ssd_norm_w, w_out,
                w_ff1, b_ff1, w_ff2, b_ff2, final_norm_w,
                prompt_tl=512, ffn_tl=1024, past_len=4096)
```

```python
import functools

import jax
import jax.numpy as jnp
from jax import lax
from jax.experimental import pallas as pl
from jax.experimental.pallas import tpu as pltpu

D_MODEL = 1024
POOL_WIDTH = 1024
POOL_GROUPS = 4
POOL_GROUP_DIM = POOL_WIDTH // POOL_GROUPS
POOL_WINDOWS = (2, 4, 8, 16)
POOL_HIST = 15
POOL_HIST_ROWS = 16
SSD_WIDTH = 1024
SSD_HEAD_DIM = 64
SSD_HEADS = 16
SSD_GROUPS = 2
SSD_GROUP_WIDTH = SSD_WIDTH // SSD_GROUPS
SSD_STATE = 128
SSD_CONV = 4
CONV_HIST = SSD_CONV - 1
CONV_HIST_ROWS = 8
CONV_DIM = SSD_WIDTH + 2 * SSD_GROUPS * SSD_STATE
D_IN_PROJ = POOL_WIDTH + SSD_WIDTH + CONV_DIM + SSD_HEADS
D_FF = 4 * D_MODEL
EPS = 1e-6

V7X_VMEM_LIMIT_BYTES = 56 * 1024 * 1024
LANES = 128
LOG2_E = 1.4426950408889634
SSD_CHUNK = 128
PROJ_COL_BLOCK = 256
WORK_NORM = 0
WORK_POOL_GROUP = (130, 270, 400, 400)
WORK_CONV_BLOCK = 220
WORK_SSD_PRELUDE = 300
WORK_SSD_HEAD_PAIR = 160
WORK_SSD_TAIL = 350
WORK_GATE = 400
SIDE_PER_MAIN = 0.95

_BF16 = jnp.bfloat16
_F32 = jnp.float32


def _dot(a, b):
    return jnp.dot(a, b, preferred_element_type=_F32)


def _dot_nt(a, b):
    return lax.dot_general(a, b, (((1,), (1,)), ((), ())), preferred_element_type=_F32)


def _dot_tn(a, b):
    return lax.dot_general(a, b, (((0,), (0,)), ((), ())), preferred_element_type=_F32)


def _split3(v):
    v1 = v.astype(_BF16)
    r1 = v - v1.astype(_F32)
    v2 = r1.astype(_BF16)
    v3 = (r1 - v2.astype(_F32)).astype(_BF16)
    return v1, v2, v3


def _split2(v):
    v1 = v.astype(_BF16)
    v2 = (v - v1.astype(_F32)).astype(_BF16)
    return v1, v2


def _silu(v):
    hv = 0.5 * v
    return hv + hv * jnp.tanh(hv)


def _softplus(v):
    return jnp.maximum(v, 0.0) + jnp.log1p(jnp.exp(-jnp.abs(v)))


def _rms(v):
    return v * lax.rsqrt(jnp.mean(v * v, axis=-1, keepdims=True) + EPS)


def _shift_rows(v, k):
    return pltpu.roll(v, k, axis=0)


def _ada_kernel(c_ref, w_ref, b_ref, o_ref):
    s = _silu(c_ref[...]).astype(_BF16)
    o_ref[...] = _dot(s, w_ref[...].astype(_BF16)) + b_ref[...]


def _ada(c_all, w_ada, b_ada):
    nb = c_all.shape[0]
    n_out = w_ada.shape[1]
    blk = D_MODEL
    return pl.pallas_call(
        _ada_kernel,
        grid=(n_out // blk,),
        in_specs=[
            pl.BlockSpec((nb, D_MODEL), lambda j: (0, 0)),
            pl.BlockSpec((D_MODEL, blk), lambda j: (0, j)),
            pl.BlockSpec((1, blk), lambda j: (0, j)),
        ],
        out_specs=pl.BlockSpec((nb, blk), lambda j: (0, j)),
        out_shape=jax.ShapeDtypeStruct((nb, n_out), _F32),
        compiler_params=pltpu.CompilerParams(dimension_semantics=("arbitrary",)),
        name="ada",
    )(c_all, w_ada, b_ada.reshape(1, n_out))


def _interleave(main, side, side_per_main):
    main_done = 0.0
    side_done = 0.0
    side_left = True
    for work in main:
        main_done += work
        while side_left and side_done < main_done * side_per_main:
            cost = next(side, None)
            side_left = cost is not None
            side_done += cost or 0.0
    for _ in side:
        pass


def _ssd_decay_terms(dtr, dtb, neg_a):
    q = dtr.shape[0]
    dt_c = _softplus(dtr + dtb)
    a_c = dt_c * neg_a
    ri = lax.broadcasted_iota(jnp.int32, (q, q), 0)
    ci = lax.broadcasted_iota(jnp.int32, (q, q), 1)
    tri = jnp.where(ri >= ci, 1.0, 0.0).astype(_BF16)
    a1, a2, a3 = _split3(a_c)
    acs_c = _dot(jnp.concatenate([tri, tri, tri], axis=1),
                 jnp.concatenate([a1, a2, a3], axis=0))

    side = max(q, LANES)
    acs2_c = acs_c * LOG2_E
    cols = jnp.concatenate([dt_c, acs2_c, jnp.zeros((q, LANES - 2 * SSD_HEADS), _F32)], axis=1)
    if q < side:
        cols = jnp.concatenate([cols, jnp.zeros((side - q, LANES), _F32)], axis=0)
    rows = cols.T
    dt_r = rows[0:SSD_HEADS, 0:q]
    acs2_r = rows[SSD_HEADS:2 * SSD_HEADS, 0:q]
    eacs = jnp.exp(acs_c)
    w_state = dt_c * jnp.exp(acs_c[q - 1:q, :] - acs_c)
    return dt_r, acs2_c, acs2_r, eacs, w_state


def _ssd_chunk(xs, bm, cm, decay_terms, st_ref, dskip, out):
    q = xs.shape[0]
    dt_r, acs2_c, acs2_r, eacs, w_state = decay_terms

    hi = lax.broadcasted_iota(jnp.int32, (2 * SSD_HEADS, SSD_WIDTH), 0) % SSD_HEADS
    chn = lax.broadcasted_iota(jnp.int32, (2 * SSD_HEADS, SSD_WIDTH), 1)
    expand2 = jnp.where((chn >= hi * SSD_HEAD_DIM) & (chn < (hi + 1) * SSD_HEAD_DIM),
                        1.0, 0.0).astype(_BF16)

    def _expand_heads(v):
        return _dot(jnp.concatenate(_split2(v), axis=1), expand2)

    eacs_rep = _expand_heads(eacs)
    w_state_rep = _expand_heads(w_state)
    chunk_decay_rep = eacs_rep[q - 1:q, :]
    ri = lax.broadcasted_iota(jnp.int32, (q, q), 0)
    ci = lax.broadcasted_iota(jnp.int32, (q, q), 1)
    causal = ri >= ci

    lane = lax.broadcasted_iota(jnp.int32, (q, 2 * SSD_HEAD_DIM), 1)
    first_head = lane < SSD_HEAD_DIM
    bgs = [bm[:, g * SSD_STATE:(g + 1) * SSD_STATE].astype(_BF16) for g in range(SSD_GROUPS)]
    cgs = [cm[:, g * SSD_STATE:(g + 1) * SSD_STATE].astype(_BF16) for g in range(SSD_GROUPS)]
    cbm = [jnp.where(causal, _dot_nt(cgs[g], bgs[g]), 0.0) for g in range(SSD_GROUPS)]
    yield WORK_SSD_PRELUDE
    yd_parts = []
    for p in range(SSD_HEADS // 2):
        g = (2 * p) // (SSD_HEADS // SSD_GROUPS)
        ms = []
        for hh in (2 * p, 2 * p + 1):
            seg2 = jnp.minimum(acs2_c[:, hh:hh + 1] - acs2_r[hh:hh + 1, :], 0.0)
            ms.append((cbm[g] * jnp.exp2(seg2) * dt_r[hh:hh + 1, :]).astype(_BF16))
        xp = xs[:, p * 2 * SSD_HEAD_DIM:(p + 1) * 2 * SSD_HEAD_DIM]
        rhs = jnp.concatenate([jnp.where(first_head, xp, 0.0), jnp.where(first_head, 0.0, xp)],
                              axis=0).astype(_BF16)
        yd_parts.append(_dot(jnp.concatenate(ms, axis=1), rhs))
        yield WORK_SSD_HEAD_PAIR
    y_diag = jnp.concatenate(yd_parts, axis=1)

    xw = (xs * w_state_rep).astype(_BF16)
    yo_parts = []
    for g in range(SSD_GROUPS):
        gc = slice(g * SSD_GROUP_WIDTH, (g + 1) * SSD_GROUP_WIDTH)
        st_g = st_ref[:, gc]
        yo_parts.append(_dot(cgs[g], st_g.astype(_BF16)))
        st_ref[:, gc] = st_g * chunk_decay_rep[:, gc] + _dot_tn(bgs[g], xw[:, gc])
    y_off = jnp.concatenate(yo_parts, axis=1) * eacs_rep
    out.append(y_diag + y_off + dskip * xs)
    yield WORK_SSD_TAIL


class _Stage:
    def __init__(self, ubuf, z, cbuf, dtr, y):
        self.ubuf = ubuf
        self.z = z
        self.cbuf = cbuf
        self.dtr = dtr
        self.y = y


def _mixer_kernel(xa_ref, xc_ref, moda_ref, modc_ref, ph_ref, ch_ref, h0_ref,
                  nw_ref, win_ref,
                  pw_ref, pb_ref, ps_ref, cw_ref, cb_ref,
                  dtb_ref, alog_ref, dskip_ref, snw_ref, wout_ref,
                  x1_ref, npool_ref, nconv_ref, nssm_ref,
                  u0s, z0s, c0s, d0s, y0s, u1s, z1s, c1s, d1s, y1s, st_ref, hb_ref,
                  *, tl, q, pos0, n_steps, n_tiles):
    i = pl.program_id(0)
    b_step = lax.rem(i - 1, n_steps)
    stages = (_Stage(u0s, z0s, c0s, d0s, y0s), _Stage(u1s, z1s, c1s, d1s, y1s))
    proj_blk = PROJ_COL_BLOCK if tl >= SSD_CHUNK else 2 * PROJ_COL_BLOCK

    @pl.when(i == 0)
    def _zero_first_stage():
        for ref in (u1s, z1s, c1s, d1s, y1s, st_ref):
            ref[...] = jnp.zeros(ref.shape, ref.dtype)

    def phase_a(sa):
        blk = proj_blk
        cost = tl * blk // 256
        o_z, o_xbc, o_dt = POOL_WIDTH, POOL_WIDTH + SSD_WIDTH, POOL_WIDTH + SSD_WIDTH + CONV_DIM
        for c0 in range(0, POOL_WIDTH, blk):
            sa.ubuf[POOL_HIST_ROWS:POOL_HIST_ROWS + tl, c0:c0 + blk] = \
                _dot(hb_ref[...], win_ref[:, c0:c0 + blk])
            yield cost
        for c0 in range(0, CONV_DIM, blk):
            sa.cbuf[CONV_HIST_ROWS:CONV_HIST_ROWS + tl, c0:c0 + blk] = \
                _dot(hb_ref[...], win_ref[:, o_xbc + c0:o_xbc + c0 + blk])
            yield cost
        sa.dtr[...] = _dot(hb_ref[...], win_ref[:, o_dt:o_dt + SSD_HEADS])
        for c0 in range(0, SSD_WIDTH, blk):
            sa.z[:, c0:c0 + blk] = _dot(hb_ref[...], win_ref[:, o_z + c0:o_z + c0 + blk])
            yield cost

    def phase_c(sc):
        blk = proj_blk
        g1 = modc_ref[0][2:3]
        for c0 in range(0, D_MODEL, blk):
            mix = _dot(sc.y[...], wout_ref[:, c0:c0 + blk])
            x1_ref[0, :, c0:c0 + blk] = xc_ref[0, :, c0:c0 + blk] + g1[:, c0:c0 + blk] * mix
            yield 2 * tl * blk // 256

    def side_pieces(sa, sc):
        yield from phase_c(sc)
        yield from phase_a(sa)

    def phase_b(sb, sa):
        ts = tl / 256.0
        yield WORK_NORM * ts

        neg_a = -jnp.exp(alog_ref[...])
        dtr = sb.dtr[...]
        decay_terms = [_ssd_decay_terms(dtr[c * q:(c + 1) * q, :], dtb_ref[...], neg_a)
                       for c in range(tl // q)]

        pos = (lax.broadcasted_iota(jnp.int32, (tl, POOL_GROUP_DIM), 0)
               + (pos0 + b_step * tl)).astype(_F32)
        ya_parts = []
        for g, w in enumerate(POOL_WINDOWS):
            lo = g * POOL_GROUP_DIM
            ug = sb.ubuf[:, lo:lo + POOL_GROUP_DIM]
            wsum = ug
            k = 1
            while k < w:
                wsum = wsum + _shift_rows(wsum, k)
                k *= 2
            cnt = jnp.minimum(jnp.float32(w), pos + 1.0)
            pooled = wsum[POOL_HIST_ROWS:, :] / cnt - ug[POOL_HIST_ROWS:, :]
            ya_parts.append(_dot(pooled.astype(_BF16), pw_ref[g]))
            yield WORK_POOL_GROUP[g] * ts
        ya = (jnp.concatenate(ya_parts, axis=1) + pb_ref[...]) * ps_ref[...]

        act_parts = []
        for c0 in range(0, CONV_DIM, PROJ_COL_BLOCK):
            cc = slice(c0, c0 + PROJ_COL_BLOCK)
            full = sb.cbuf[:, cc]
            prev = _shift_rows(full, 1)
            near = full * cw_ref[3:4, cc] + prev * cw_ref[2:3, cc]
            far = full * cw_ref[1:2, cc] + prev * cw_ref[0:1, cc]
            conv = cb_ref[:, cc] + near + _shift_rows(far, 2)
            act_parts.append(_silu(conv[CONV_HIST_ROWS:, :]))
            yield WORK_CONV_BLOCK * ts
        xbc_act = jnp.concatenate(act_parts, axis=1)
        b_off = SSD_WIDTH
        c_off = SSD_WIDTH + SSD_GROUPS * SSD_STATE

        y_parts = []
        for c in range(tl // q):
            r = slice(c * q, (c + 1) * q)
            yield from _ssd_chunk(xbc_act[r, 0:SSD_WIDTH], xbc_act[r, b_off:c_off],
                                  xbc_act[r, c_off:CONV_DIM], decay_terms[c], st_ref,
                                  dskip_ref[...], y_parts)
        y = jnp.concatenate(y_parts, axis=0) if len(y_parts) > 1 else y_parts[0]

        y = y * _silu(sb.z[...])
        yb = jnp.concatenate(
            [_rms(y[:, g * SSD_GROUP_WIDTH:(g + 1) * SSD_GROUP_WIDTH]) for g in range(SSD_GROUPS)],
            axis=1) * snw_ref[...]
        sa.y[:, 0:POOL_WIDTH] = ya.astype(_BF16)
        sa.y[:, POOL_WIDTH:POOL_WIDTH + SSD_WIDTH] = yb.astype(_BF16)

        sa.ubuf[0:POOL_HIST_ROWS, :] = sb.ubuf[tl:tl + POOL_HIST_ROWS, :]
        sa.cbuf[0:CONV_HIST_ROWS, :] = sb.cbuf[tl:tl + CONV_HIST_ROWS, :]
        yield WORK_GATE * ts

    for parity in (0, 1):
        @pl.when(lax.rem(i, 2) == parity)
        def _step(parity=parity):
            sa, sb = stages[parity], stages[1 - parity]

            @pl.when(b_step == 0)
            def _load_history():
                sb.ubuf[0:POOL_HIST_ROWS, :] = ph_ref[0]
                sb.cbuf[0:CONV_HIST_ROWS, :] = ch_ref[0]
                st_ref[...] = h0_ref[0].reshape(SSD_WIDTH, SSD_STATE).T

            x = xa_ref[0]
            mod = moda_ref[0]
            sh1, sc1 = mod[0:1], mod[1:2]
            h = _rms(x) * (nw_ref[...] * (1.0 + sc1)) + sh1
            hb_ref[...] = h.astype(_BF16)

            _interleave(phase_b(sb, sa), side_pieces(sa, sb), SIDE_PER_MAIN)

            @pl.when((b_step == n_steps - 1) & (i <= n_tiles))
            def _emit_caches():
                npool_ref[0] = sb.ubuf[tl:tl + POOL_HIST_ROWS, :]
                nconv_ref[0] = sb.cbuf[tl:tl + CONV_HIST_ROWS, :]
                nssm_ref[0] = st_ref[...].T.reshape(SSD_HEADS, SSD_HEAD_DIM, SSD_STATE)


def _mixer(x, mod, pool_hist, conv_hist, h0, wts, *, pos0, tl):
    nb, seq, _ = x.shape
    assert seq % tl == 0
    n_steps = seq // tl
    n_tiles = nb * n_steps
    q = min(tl, SSD_CHUNK)
    assert tl % q == 0
    ph = jnp.pad(pool_hist, ((0, 0), (POOL_HIST_ROWS - POOL_HIST, 0), (0, 0)))
    ch = jnp.pad(conv_hist, ((0, 0), (CONV_HIST_ROWS - CONV_HIST, 0), (0, 0)))

    def const(shape):
        return pl.BlockSpec(shape, lambda i: (0,) * len(shape))

    def tile_a(i):
        return jnp.minimum(i, n_tiles - 1)

    def tile_b(i):
        return jnp.clip(i - 1, 0, n_tiles - 1)

    def tile_c(i):
        return jnp.maximum(i - 2, 0)

    in_specs = [
        pl.BlockSpec((1, tl, D_MODEL), lambda i: (tile_a(i) // n_steps, tile_a(i) % n_steps, 0)),
        pl.BlockSpec((1, tl, D_MODEL), lambda i: (tile_c(i) // n_steps, tile_c(i) % n_steps, 0)),
        pl.BlockSpec((1, 6, D_MODEL), lambda i: (tile_a(i) // n_steps, 0, 0)),
        pl.BlockSpec((1, 6, D_MODEL), lambda i: (tile_c(i) // n_steps, 0, 0)),
        pl.BlockSpec((1, POOL_HIST_ROWS, POOL_WIDTH), lambda i: (tile_b(i) // n_steps, 0, 0)),
        pl.BlockSpec((1, CONV_HIST_ROWS, CONV_DIM), lambda i: (tile_b(i) // n_steps, 0, 0)),
        pl.BlockSpec((1, SSD_HEADS, SSD_HEAD_DIM, SSD_STATE), lambda i: (tile_b(i) // n_steps, 0, 0, 0)),
        const((1, D_MODEL)),
        const((D_MODEL, D_IN_PROJ)),
        const((POOL_GROUPS, POOL_GROUP_DIM, POOL_GROUP_DIM)),
        const((1, POOL_WIDTH)),
        const((1, POOL_WIDTH)),
        const((SSD_CONV, CONV_DIM)),
        const((1, CONV_DIM)),
        const((1, SSD_HEADS)),
        const((1, SSD_HEADS)),
        const((1, SSD_WIDTH)),
        const((1, SSD_WIDTH)),
        const((POOL_WIDTH + SSD_WIDTH, D_MODEL)),
    ]
    out_specs = [
        pl.BlockSpec((1, tl, D_MODEL), lambda i: (tile_c(i) // n_steps, tile_c(i) % n_steps, 0)),
        pl.BlockSpec((1, POOL_HIST_ROWS, POOL_WIDTH), lambda i: (tile_b(i) // n_steps, 0, 0)),
        pl.BlockSpec((1, CONV_HIST_ROWS, CONV_DIM), lambda i: (tile_b(i) // n_steps, 0, 0)),
        pl.BlockSpec((1, SSD_HEADS, SSD_HEAD_DIM, SSD_STATE), lambda i: (tile_b(i) // n_steps, 0, 0, 0)),
    ]
    out_shape = [
        jax.ShapeDtypeStruct((nb, seq, D_MODEL), _F32),
        jax.ShapeDtypeStruct((nb, POOL_HIST_ROWS, POOL_WIDTH), _F32),
        jax.ShapeDtypeStruct((nb, CONV_HIST_ROWS, CONV_DIM), _F32),
        jax.ShapeDtypeStruct((nb, SSD_HEADS, SSD_HEAD_DIM, SSD_STATE), _F32),
    ]
    x1, npool, nconv, nssm = pl.pallas_call(
        functools.partial(_mixer_kernel, tl=tl, q=q, pos0=pos0, n_steps=n_steps, n_tiles=n_tiles),
        grid=(n_tiles + 2,),
        in_specs=in_specs,
        out_specs=out_specs,
        out_shape=out_shape,
        scratch_shapes=2 * [
            pltpu.VMEM((POOL_HIST_ROWS + tl, POOL_WIDTH), _F32),
            pltpu.VMEM((tl, SSD_WIDTH), _F32),
            pltpu.VMEM((CONV_HIST_ROWS + tl, CONV_DIM), _F32),
            pltpu.VMEM((tl, SSD_HEADS), _F32),
            pltpu.VMEM((tl, POOL_WIDTH + SSD_WIDTH), _BF16),
        ] + [pltpu.VMEM((SSD_STATE, SSD_WIDTH), _F32), pltpu.VMEM((tl, D_MODEL), _BF16)],
        compiler_params=pltpu.CompilerParams(
            dimension_semantics=("arbitrary",),
            vmem_limit_bytes=V7X_VMEM_LIMIT_BYTES),
        name="mixer",
    )(x, x, mod, mod, ph, ch, h0, *wts)
    return x1, npool[:, POOL_HIST_ROWS - POOL_HIST:], nconv[:, CONV_HIST_ROWS - CONV_HIST:], nssm


def _ffn_kernel(x_ref, mod_ref, nw_ref, w1_ref, b1_ref, w2_ref, b2_ref, fw_ref, y_ref, *, ff_chunk):
    tb, tl, _ = x_ref.shape
    x3 = x_ref[...]
    mod = mod_ref[...]
    sh2, sc2, g2 = mod[:, 3:4, :], mod[:, 4:5, :], mod[:, 5:6, :]
    h3 = (_rms(x3) * nw_ref[...]) * (1.0 + sc2) + sh2
    hb = h3.reshape(tb * tl, D_MODEL).astype(_BF16)
    acc = None
    for j in range(D_FF // ff_chunk):
        cols = slice(j * ff_chunk, (j + 1) * ff_chunk)
        f = jnp.maximum(_dot(hb, w1_ref[:, cols]) + b1_ref[:, cols], 0.0)
        part = _dot((f * f).astype(_BF16), w2_ref[cols, :])
        acc = part if acc is None else acc + part
    f2 = (acc + b2_ref[...]).reshape(tb, tl, D_MODEL)
    x2 = x3 + g2 * f2
    y_ref[...] = _rms(x2) * fw_ref[...]


def _ffn(x1, mod, wts, *, tb, tl):
    nb, seq, _ = x1.shape
    assert nb % tb == 0 and seq % tl == 0

    def const(shape):
        return pl.BlockSpec(shape, lambda b, s: (0,) * len(shape))

    return pl.pallas_call(
        functools.partial(_ffn_kernel, ff_chunk=1024),
        grid=(nb // tb, seq // tl),
        in_specs=[
            pl.BlockSpec((tb, tl, D_MODEL), lambda b, s: (b, s, 0)),
            pl.BlockSpec((tb, 6, D_MODEL), lambda b, s: (b, 0, 0)),
            const((1, D_MODEL)),
            const((D_MODEL, D_FF)),
            const((1, D_FF)),
            const((D_FF, D_MODEL)),
            const((1, D_MODEL)),
            const((1, D_MODEL)),
        ],
        out_specs=pl.BlockSpec((tb, tl, D_MODEL), lambda b, s: (b, s, 0)),
        out_shape=jax.ShapeDtypeStruct((nb, seq, D_MODEL), _F32),
        compiler_params=pltpu.CompilerParams(
            dimension_semantics=("arbitrary", "arbitrary"),
            vmem_limit_bytes=V7X_VMEM_LIMIT_BYTES),
        name="ffn",
    )(x1, mod, *wts)


def _layer_weights(l, norm_mix_w, norm_ffn_w, w_in, pool_w, pool_b, pool_scale, conv_w, conv_b,
                   dt_bias, a_log, d_skip, ssd_norm_w, w_out, w_ff1, b_ff1, w_ff2, b_ff2):
    mixer_w = (
        norm_mix_w[l].reshape(1, D_MODEL),
        w_in[l].astype(_BF16),
        pool_w[l].astype(_BF16),
        pool_b[l].reshape(1, POOL_WIDTH),
        pool_scale[l].reshape(1, POOL_WIDTH),
        conv_w[l],
        conv_b[l].reshape(1, CONV_DIM),
        dt_bias[l].reshape(1, SSD_HEADS),
        a_log[l].reshape(1, SSD_HEADS),
        jnp.repeat(d_skip[l], SSD_HEAD_DIM).reshape(1, SSD_WIDTH),
        ssd_norm_w[l].reshape(1, SSD_WIDTH),
        w_out[l].astype(_BF16),
    )
    ffn_w = (
        norm_ffn_w[l].reshape(1, D_MODEL),
        w_ff1[l].astype(_BF16),
        b_ff1[l].reshape(1, D_FF),
        w_ff2[l].astype(_BF16),
        b_ff2[l].reshape(1, D_MODEL),
    )
    return mixer_w, ffn_w


def _run(x_prompt, x_sample, cache_pool, cache_conv, state_ssm, c_prompt, c_sample,
         norm_mix_w, norm_ffn_w, w_ada, b_ada, w_in, pool_w, pool_b, pool_scale,
         conv_w, conv_b, dt_bias, a_log, d_skip, ssd_norm_w, w_out,
         w_ff1, b_ff1, w_ff2, b_ff2, final_norm_w, *, prompt_tl, ffn_tl, past_len):
    depth = w_in.shape[0]
    assert depth == 1, "the final norm is fused into the (single) layer's ffn call"
    nbp, seq_p, _ = x_prompt.shape
    nbs, seq_s, _ = x_sample.shape
    fw = final_norm_w.reshape(1, D_MODEL)
    c_all = jnp.concatenate([c_prompt, c_sample], axis=0)
    xp, xs = x_prompt, x_sample
    outs = {k: [] for k in ("pp", "cp", "sp", "ps", "cs", "ss")}
    for l in range(depth):
        mod = _ada(c_all, w_ada[l], b_ada[l]).reshape(nbp + nbs, 6, D_MODEL)
        mod_p, mod_s = mod[:nbp], mod[nbp:]
        mixer_w, ffn_w = _layer_weights(l, norm_mix_w, norm_ffn_w, w_in, pool_w, pool_b, pool_scale,
                                        conv_w, conv_b, dt_bias, a_log, d_skip, ssd_norm_w, w_out,
                                        w_ff1, b_ff1, w_ff2, b_ff2)
        zp = jnp.zeros((nbp, POOL_HIST, POOL_WIDTH), _F32)
        zc = jnp.zeros((nbp, CONV_HIST, CONV_DIM), _F32)
        zh = jnp.zeros((nbp, SSD_HEADS, SSD_HEAD_DIM, SSD_STATE), _F32)
        xp, npool, nconv, nssm = _mixer(xp, mod_p, zp, zc, zh, mixer_w, pos0=0, tl=prompt_tl)
        xp = _ffn(xp, mod_p, ffn_w + (fw,), tb=1, tl=ffn_tl)
        outs["pp"].append(npool); outs["cp"].append(nconv); outs["sp"].append(nssm)
        xs, npool, nconv, nssm = _mixer(xs, mod_s, cache_pool[l], cache_conv[l], state_ssm[l], mixer_w,
                                        pos0=past_len, tl=seq_s)
        xs = _ffn(xs, mod_s, ffn_w + (fw,), tb=nbs, tl=seq_s)
        outs["ps"].append(npool); outs["cs"].append(nconv); outs["ss"].append(nssm)
    st = lambda k: outs[k][0][None]
    return (xp, xs, st("pp"), st("cp"), st("sp"), st("ps"), st("cs"), st("ss"))


def kernel(x_prompt, x_sample, cache_pool, cache_conv, state_ssm, c_prompt, c_sample, norm_mix_w, norm_ffn_w, w_ada, b_ada, w_in, pool_w, pool_b, pool_scale, conv_w, conv_b, dt_bias, a_log, d_skip, ssd_norm_w, w_out, w_ff1, b_ff1, w_ff2, b_ff2, final_norm_w):
    return _run(x_prompt, x_sample, cache_pool, cache_conv, state_ssm, c_prompt, c_sample,
                norm_mix_w, norm_ffn_w, w_ada, b_ada, w_in, pool_w, pool_b, pool_scale,
                conv_w, conv_b, dt_bias, a_log, d_skip, ssd_norm_w, w_out,
                w_ff1, b_ff1, w_ff2, b_ff2, final_norm_w,
                prompt_tl=512, ffn_tl=1024, past_len=4096)
```

```python
import functools

import jax
import jax.numpy as jnp
from jax import lax
from jax.experimental import pallas as pl
from jax.experimental.pallas import tpu as pltpu

D_MODEL = 1024
POOL_WIDTH = 1024
POOL_GROUPS = 4
POOL_GROUP_DIM = POOL_WIDTH // POOL_GROUPS
POOL_WINDOWS = (2, 4, 8, 16)
POOL_HIST = 15
POOL_HIST_ROWS = 16
SSD_WIDTH = 1024
SSD_HEAD_DIM = 64
SSD_HEADS = 16
SSD_GROUPS = 2
SSD_GROUP_WIDTH = SSD_WIDTH // SSD_GROUPS
SSD_STATE = 128
SSD_CONV = 4
CONV_HIST = SSD_CONV - 1
CONV_HIST_ROWS = 8
CONV_DIM = SSD_WIDTH + 2 * SSD_GROUPS * SSD_STATE
D_IN_PROJ = POOL_WIDTH + SSD_WIDTH + CONV_DIM + SSD_HEADS
D_FF = 4 * D_MODEL
EPS = 1e-6

V7X_VMEM_LIMIT_BYTES = 56 * 1024 * 1024
LANES = 128
LOG2_E = 1.4426950408889634
SSD_CHUNK = 128
PROJ_COL_BLOCK = 256
WORK_NORM = 0
WORK_POOL_GROUP = (130, 270, 400, 400)
WORK_CONV_BLOCK = 220
WORK_SSD_PRELUDE = 300
WORK_SSD_HEAD_PAIR = 160
WORK_SSD_TAIL = 350
WORK_GATE = 400
SIDE_PER_MAIN = 0.95

_BF16 = jnp.bfloat16
_F32 = jnp.float32


def _dot(a, b):
    return jnp.dot(a, b, preferred_element_type=_F32)


def _dot_nt(a, b):
    return lax.dot_general(a, b, (((1,), (1,)), ((), ())), preferred_element_type=_F32)


def _dot_tn(a, b):
    return lax.dot_general(a, b, (((0,), (0,)), ((), ())), preferred_element_type=_F32)


def _split3(v):
    v1 = v.astype(_BF16)
    r1 = v - v1.astype(_F32)
    v2 = r1.astype(_BF16)
    v3 = (r1 - v2.astype(_F32)).astype(_BF16)
    return v1, v2, v3


def _split2(v):
    v1 = v.astype(_BF16)
    v2 = (v - v1.astype(_F32)).astype(_BF16)
    return v1, v2


def _silu(v):
    hv = 0.5 * v
    return hv + hv * jnp.tanh(hv)


def _softplus(v):
    return jnp.maximum(v, 0.0) + jnp.log1p(jnp.exp(-jnp.abs(v)))


def _rms(v):
    return v * lax.rsqrt(jnp.mean(v * v, axis=-1, keepdims=True) + EPS)


def _shift_rows(v, k):
    return pltpu.roll(v, k, axis=0)


def _ada_kernel(c_ref, w_ref, b_ref, o_ref):
    s = _silu(c_ref[...]).astype(_BF16)
    o_ref[...] = _dot(s, w_ref[...].astype(_BF16)) + b_ref[...]


def _ada(c_all, w_ada, b_ada):
    nb = c_all.shape[0]
    n_out = w_ada.shape[1]
    blk = D_MODEL
    return pl.pallas_call(
        _ada_kernel,
        grid=(n_out // blk,),
        in_specs=[
            pl.BlockSpec((nb, D_MODEL), lambda j: (0, 0)),
            pl.BlockSpec((D_MODEL, blk), lambda j: (0, j)),
            pl.BlockSpec((1, blk), lambda j: (0, j)),
        ],
        out_specs=pl.BlockSpec((nb, blk), lambda j: (0, j)),
        out_shape=jax.ShapeDtypeStruct((nb, n_out), _F32),
        compiler_params=pltpu.CompilerParams(dimension_semantics=("arbitrary",)),
        name="ada",
    )(c_all, w_ada, b_ada.reshape(1, n_out))


def _interleave(main, side, side_per_main):
    main_done = 0.0
    side_done = 0.0
    side_left = True
    for work in main:
        main_done += work
        while side_left and side_done < main_done * side_per_main:
            cost = next(side, None)
            side_left = cost is not None
            side_done += cost or 0.0
    for _ in side:
        pass


def _ssd_decay_terms(dtr, dtb, neg_a):
    q = dtr.shape[0]
    dt_c = _softplus(dtr + dtb)
    a_c = dt_c * neg_a
    ri = lax.broadcasted_iota(jnp.int32, (q, q), 0)
    ci = lax.broadcasted_iota(jnp.int32, (q, q), 1)
    tri = jnp.where(ri >= ci, 1.0, 0.0).astype(_BF16)
    a1, a2, a3 = _split3(a_c)
    acs_c = _dot(jnp.concatenate([tri, tri, tri], axis=1),
                 jnp.concatenate([a1, a2, a3], axis=0))

    side = max(q, LANES)
    acs2_c = acs_c * LOG2_E
    cols = jnp.concatenate([dt_c, acs2_c, jnp.zeros((q, LANES - 2 * SSD_HEADS), _F32)], axis=1)
    if q < side:
        cols = jnp.concatenate([cols, jnp.zeros((side - q, LANES), _F32)], axis=0)
    rows = cols.T
    dt_r = rows[0:SSD_HEADS, 0:q]
    acs2_r = rows[SSD_HEADS:2 * SSD_HEADS, 0:q]
    eacs = jnp.exp(acs_c)
    w_state = dt_c * jnp.exp(acs_c[q - 1:q, :] - acs_c)
    return dt_r, acs2_c, acs2_r, eacs, w_state


def _ssd_chunk(xs, bm, cm, decay_terms, st_ref, dskip, out):
    q = xs.shape[0]
    dt_r, acs2_c, acs2_r, eacs, w_state = decay_terms

    hi = lax.broadcasted_iota(jnp.int32, (2 * SSD_HEADS, SSD_WIDTH), 0) % SSD_HEADS
    chn = lax.broadcasted_iota(jnp.int32, (2 * SSD_HEADS, SSD_WIDTH), 1)
    expand2 = jnp.where((chn >= hi * SSD_HEAD_DIM) & (chn < (hi + 1) * SSD_HEAD_DIM),
                        1.0, 0.0).astype(_BF16)

    def _expand_heads(v):
        return _dot(jnp.concatenate(_split2(v), axis=1), expand2)

    eacs_rep = _expand_heads(eacs)
    w_state_rep = _expand_heads(w_state)
    chunk_decay_rep = eacs_rep[q - 1:q, :]
    ri = lax.broadcasted_iota(jnp.int32, (q, q), 0)
    ci = lax.broadcasted_iota(jnp.int32, (q, q), 1)
    causal = ri >= ci

    lane = lax.broadcasted_iota(jnp.int32, (q, 2 * SSD_HEAD_DIM), 1)
    first_head = lane < SSD_HEAD_DIM
    bgs = [bm[:, g * SSD_STATE:(g + 1) * SSD_STATE].astype(_BF16) for g in range(SSD_GROUPS)]
    cgs = [cm[:, g * SSD_STATE:(g + 1) * SSD_STATE].astype(_BF16) for g in range(SSD_GROUPS)]
    cbm = [jnp.where(causal, _dot_nt(cgs[g], bgs[g]), 0.0) for g in range(SSD_GROUPS)]
    yield WORK_SSD_PRELUDE
    yd_parts = []
    for p in range(SSD_HEADS // 2):
        g = (2 * p) // (SSD_HEADS // SSD_GROUPS)
        ms = []
        for hh in (2 * p, 2 * p + 1):
            seg2 = jnp.minimum(acs2_c[:, hh:hh + 1] - acs2_r[hh:hh + 1, :], 0.0)
            ms.append((cbm[g] * jnp.exp2(seg2) * dt_r[hh:hh + 1, :]).astype(_BF16))
        xp = xs[:, p * 2 * SSD_HEAD_DIM:(p + 1) * 2 * SSD_HEAD_DIM]
        rhs = jnp.concatenate([jnp.where(first_head, xp, 0.0), jnp.where(first_head, 0.0, xp)],
                              axis=0).astype(_BF16)
        yd_parts.append(_dot(jnp.concatenate(ms, axis=1), rhs))
        yield WORK_SSD_HEAD_PAIR
    y_diag = jnp.concatenate(yd_parts, axis=1)

    xw = (xs * w_state_rep).astype(_BF16)
    yo_parts = []
    for g in range(SSD_GROUPS):
        gc = slice(g * SSD_GROUP_WIDTH, (g + 1) * SSD_GROUP_WIDTH)
        st_g = st_ref[:, gc]
        yo_parts.append(_dot(cgs[g], st_g.astype(_BF16)))
        st_ref[:, gc] = st_g * chunk_decay_rep[:, gc] + _dot_tn(bgs[g], xw[:, gc])
    y_off = jnp.concatenate(yo_parts, axis=1) * eacs_rep
    out.append(y_diag + y_off + dskip * xs)
    yield WORK_SSD_TAIL


class _Stage:
    def __init__(self, ubuf, z, cbuf, dtr, y):
        self.ubuf = ubuf
        self.z = z
        self.cbuf = cbuf
        self.dtr = dtr
        self.y = y


def _mixer_kernel(xa_ref, xc_ref, moda_ref, modc_ref, ph_ref, ch_ref, h0_ref,
                  nw_ref, win_ref,
                  pw_ref, pb_ref, ps_ref, cw_ref, cb_ref,
                  dtb_ref, alog_ref, dskip_ref, snw_ref, wout_ref,
                  x1_ref, npool_ref, nconv_ref, nssm_ref,
                  u0s, z0s, c0s, d0s, y0s, u1s, z1s, c1s, d1s, y1s, st_ref, hb_ref,
                  *, tl, q, pos0, n_steps, n_tiles):
    i = pl.program_id(0)
    b_step = lax.rem(i - 1, n_steps)
    stages = (_Stage(u0s, z0s, c0s, d0s, y0s), _Stage(u1s, z1s, c1s, d1s, y1s))
    proj_blk = PROJ_COL_BLOCK if tl >= SSD_CHUNK else 2 * PROJ_COL_BLOCK

    @pl.when(i == 0)
    def _zero_first_stage():
        for ref in (u1s, z1s, c1s, d1s, y1s, st_ref):
            ref[...] = jnp.zeros(ref.shape, ref.dtype)

    def phase_a(sa):
        blk = proj_blk
        cost = tl * blk // 256
        o_z, o_xbc, o_dt = POOL_WIDTH, POOL_WIDTH + SSD_WIDTH, POOL_WIDTH + SSD_WIDTH + CONV_DIM
        for c0 in range(0, POOL_WIDTH, blk):
            sa.ubuf[POOL_HIST_ROWS:POOL_HIST_ROWS + tl, c0:c0 + blk] = \
                _dot(hb_ref[...], win_ref[:, c0:c0 + blk])
            yield cost
        for c0 in range(0, CONV_DIM, blk):
            sa.cbuf[CONV_HIST_ROWS:CONV_HIST_ROWS + tl, c0:c0 + blk] = \
                _dot(hb_ref[...], win_ref[:, o_xbc + c0:o_xbc + c0 + blk])
            yield cost
        sa.dtr[...] = _dot(hb_ref[...], win_ref[:, o_dt:o_dt + SSD_HEADS])
        for c0 in range(0, SSD_WIDTH, blk):
            sa.z[:, c0:c0 + blk] = _dot(hb_ref[...], win_ref[:, o_z + c0:o_z + c0 + blk])
            yield cost

    def phase_c(sc):
        blk = proj_blk
        g1 = modc_ref[0][2:3]
        for c0 in range(0, D_MODEL, blk):
            mix = _dot(sc.y[...], wout_ref[:, c0:c0 + blk])
            x1_ref[0, :, c0:c0 + blk] = xc_ref[0, :, c0:c0 + blk] + g1[:, c0:c0 + blk] * mix
            yield 2 * tl * blk // 256

    def side_pieces(sa, sc):
        yield from phase_c(sc)
        yield from phase_a(sa)

    def phase_b(sb, sa):
        ts = tl / 256.0
        yield WORK_NORM * ts

        neg_a = -jnp.exp(alog_ref[...])
        dtr = sb.dtr[...]
        decay_terms = [_ssd_decay_terms(dtr[c * q:(c + 1) * q, :], dtb_ref[...], neg_a)
                       for c in range(tl // q)]

        pos = (lax.broadcasted_iota(jnp.int32, (tl, POOL_GROUP_DIM), 0)
               + (pos0 + b_step * tl)).astype(_F32)
        ya_parts = []
        for g, w in enumerate(POOL_WINDOWS):
            lo = g * POOL_GROUP_DIM
            ug = sb.ubuf[:, lo:lo + POOL_GROUP_DIM]
            wsum = ug
            k = 1
            while k < w:
                wsum = wsum + _shift_rows(wsum, k)
                k *= 2
            cnt = jnp.minimum(jnp.float32(w), pos + 1.0)
            pooled = wsum[POOL_HIST_ROWS:, :] / cnt - ug[POOL_HIST_ROWS:, :]
            ya_parts.append(_dot(pooled.astype(_BF16), pw_ref[g]))
            yield WORK_POOL_GROUP[g] * ts
        ya = (jnp.concatenate(ya_parts, axis=1) + pb_ref[...]) * ps_ref[...]

        act_parts = []
        for c0 in range(0, CONV_DIM, PROJ_COL_BLOCK):
            cc = slice(c0, c0 + PROJ_COL_BLOCK)
            full = sb.cbuf[:, cc]
            prev = _shift_rows(full, 1)
            near = full * cw_ref[3:4, cc] + prev * cw_ref[2:3, cc]
            far = full * cw_ref[1:2, cc] + prev * cw_ref[0:1, cc]
            conv = cb_ref[:, cc] + near + _shift_rows(far, 2)
            act_parts.append(_silu(conv[CONV_HIST_ROWS:, :]))
            yield WORK_CONV_BLOCK * ts
        xbc_act = jnp.concatenate(act_parts, axis=1)
        b_off = SSD_WIDTH
        c_off = SSD_WIDTH + SSD_GROUPS * SSD_STATE

        y_parts = []
        for c in range(tl // q):
            r = slice(c * q, (c + 1) * q)
            yield from _ssd_chunk(xbc_act[r, 0:SSD_WIDTH], xbc_act[r, b_off:c_off],
                                  xbc_act[r, c_off:CONV_DIM], decay_terms[c], st_ref,
                                  dskip_ref[...], y_parts)
        y = jnp.concatenate(y_parts, axis=0) if len(y_parts) > 1 else y_parts[0]

        y = y * _silu(sb.z[...])
        yb = jnp.concatenate(
            [_rms(y[:, g * SSD_GROUP_WIDTH:(g + 1) * SSD_GROUP_WIDTH]) for g in range(SSD_GROUPS)],
            axis=1) * snw_ref[...]
        sa.y[:, 0:POOL_WIDTH] = ya.astype(_BF16)
        sa.y[:, POOL_WIDTH:POOL_WIDTH + SSD_WIDTH] = yb.astype(_BF16)

        sa.ubuf[0:POOL_HIST_ROWS, :] = sb.ubuf[tl:tl + POOL_HIST_ROWS, :]
        sa.cbuf[0:CONV_HIST_ROWS, :] = sb.cbuf[tl:tl + CONV_HIST_ROWS, :]
        yield WORK_GATE * ts

    for parity in (0, 1):
        @pl.when(lax.rem(i, 2) == parity)
        def _step(parity=parity):
            sa, sb = stages[parity], stages[1 - parity]

            @pl.when(b_step == 0)
            def _load_history():
                sb.ubuf[0:POOL_HIST_ROWS, :] = ph_ref[0]
                sb.cbuf[0:CONV_HIST_ROWS, :] = ch_ref[0]
                st_ref[...] = h0_ref[0].reshape(SSD_WIDTH, SSD_STATE).T

            x = xa_ref[0]
            mod = moda_ref[0]
            sh1, sc1 = mod[0:1], mod[1:2]
            h = _rms(x) * (nw_ref[...] * (1.0 + sc1)) + sh1
            hb_ref[...] = h.astype(_BF16)

            _interleave(phase_b(sb, sa), side_pieces(sa, sb), SIDE_PER_MAIN)

            @pl.when((b_step == n_steps - 1) & (i <= n_tiles))
            def _emit_caches():
                npool_ref[0] = sb.ubuf[tl:tl + POOL_HIST_ROWS, :]
                nconv_ref[0] = sb.cbuf[tl:tl + CONV_HIST_ROWS, :]
                nssm_ref[0] = st_ref[...].T.reshape(SSD_HEADS, SSD_HEAD_DIM, SSD_STATE)


def _mixer(x, mod, pool_hist, conv_hist, h0, wts, *, pos0, tl):
    nb, seq, _ = x.shape
    assert seq % tl == 0
    n_steps = seq // tl
    n_tiles = nb * n_steps
    q = min(tl, SSD_CHUNK)
    assert tl % q == 0
    ph = jnp.pad(pool_hist, ((0, 0), (POOL_HIST_ROWS - POOL_HIST, 0), (0, 0)))
    ch = jnp.pad(conv_hist, ((0, 0), (CONV_HIST_ROWS - CONV_HIST, 0), (0, 0)))

    def const(shape):
        return pl.BlockSpec(shape, lambda i: (0,) * len(shape))

    def tile_a(i):
        return jnp.minimum(i, n_tiles - 1)

    def tile_b(i):
        return jnp.clip(i - 1, 0, n_tiles - 1)

    def tile_c(i):
        return jnp.maximum(i - 2, 0)

    assert pool_hist.shape[0] == conv_hist.shape[0] == h0.shape[0] and h0.shape[0] in (1, nb)
    shared_hist = h0.shape[0] == 1 and nb > 1

    def hist_of(i):
        return 0 if shared_hist else tile_b(i) // n_steps

    in_specs = [
        pl.BlockSpec((1, tl, D_MODEL), lambda i: (tile_a(i) // n_steps, tile_a(i) % n_steps, 0)),
        pl.BlockSpec((1, tl, D_MODEL), lambda i: (tile_c(i) // n_steps, tile_c(i) % n_steps, 0)),
        pl.BlockSpec((1, 6, D_MODEL), lambda i: (tile_a(i) // n_steps, 0, 0)),
        pl.BlockSpec((1, 6, D_MODEL), lambda i: (tile_c(i) // n_steps, 0, 0)),
        pl.BlockSpec((1, POOL_HIST_ROWS, POOL_WIDTH), lambda i: (hist_of(i), 0, 0)),
        pl.BlockSpec((1, CONV_HIST_ROWS, CONV_DIM), lambda i: (hist_of(i), 0, 0)),
        pl.BlockSpec((1, SSD_HEADS, SSD_HEAD_DIM, SSD_STATE), lambda i: (hist_of(i), 0, 0, 0)),
        const((1, D_MODEL)),
        const((D_MODEL, D_IN_PROJ)),
        const((POOL_GROUPS, POOL_GROUP_DIM, POOL_GROUP_DIM)),
        const((1, POOL_WIDTH)),
        const((1, POOL_WIDTH)),
        const((SSD_CONV, CONV_DIM)),
        const((1, CONV_DIM)),
        const((1, SSD_HEADS)),
        const((1, SSD_HEADS)),
        const((1, SSD_WIDTH)),
        const((1, SSD_WIDTH)),
        const((POOL_WIDTH + SSD_WIDTH, D_MODEL)),
    ]
    out_specs = [
        pl.BlockSpec((1, tl, D_MODEL), lambda i: (tile_c(i) // n_steps, tile_c(i) % n_steps, 0)),
        pl.BlockSpec((1, POOL_HIST_ROWS, POOL_WIDTH), lambda i: (tile_b(i) // n_steps, 0, 0)),
        pl.BlockSpec((1, CONV_HIST_ROWS, CONV_DIM), lambda i: (tile_b(i) // n_steps, 0, 0)),
        pl.BlockSpec((1, SSD_HEADS, SSD_HEAD_DIM, SSD_STATE), lambda i: (tile_b(i) // n_steps, 0, 0, 0)),
    ]
    out_shape = [
        jax.ShapeDtypeStruct((nb, seq, D_MODEL), _F32),
        jax.ShapeDtypeStruct((nb, POOL_HIST_ROWS, POOL_WIDTH), _F32),
        jax.ShapeDtypeStruct((nb, CONV_HIST_ROWS, CONV_DIM), _F32),
        jax.ShapeDtypeStruct((nb, SSD_HEADS, SSD_HEAD_DIM, SSD_STATE), _F32),
    ]
    x1, npool, nconv, nssm = pl.pallas_call(
        functools.partial(_mixer_kernel, tl=tl, q=q, pos0=pos0, n_steps=n_steps, n_tiles=n_tiles),
        grid=(n_tiles + 2,),
        in_specs=in_specs,
        out_specs=out_specs,
        out_shape=out_shape,
        scratch_shapes=2 * [
            pltpu.VMEM((POOL_HIST_ROWS + tl, POOL_WIDTH), _F32),
            pltpu.VMEM((tl, SSD_WIDTH), _F32),
            pltpu.VMEM((CONV_HIST_ROWS + tl, CONV_DIM), _F32),
            pltpu.VMEM((tl, SSD_HEADS), _F32),
            pltpu.VMEM((tl, POOL_WIDTH + SSD_WIDTH), _BF16),
        ] + [pltpu.VMEM((SSD_STATE, SSD_WIDTH), _F32), pltpu.VMEM((tl, D_MODEL), _BF16)],
        compiler_params=pltpu.CompilerParams(
            dimension_semantics=("arbitrary",),
            vmem_limit_bytes=V7X_VMEM_LIMIT_BYTES),
        name="mixer",
    )(x, x, mod, mod, ph, ch, h0, *wts)
    return x1, npool[:, POOL_HIST_ROWS - POOL_HIST:], nconv[:, CONV_HIST_ROWS - CONV_HIST:], nssm


def _ffn_kernel(x_ref, mod_ref, nw_ref, w1_ref, b1_ref, w2_ref, b2_ref, fw_ref, y_ref, *, ff_chunk):
    tb, tl, _ = x_ref.shape
    x3 = x_ref[...]
    mod = mod_ref[...]
    sh2, sc2, g2 = mod[:, 3:4, :], mod[:, 4:5, :], mod[:, 5:6, :]
    h3 = (_rms(x3) * nw_ref[...]) * (1.0 + sc2) + sh2
    hb = h3.reshape(tb * tl, D_MODEL).astype(_BF16)
    acc = None
    for j in range(D_FF // ff_chunk):
        cols = slice(j * ff_chunk, (j + 1) * ff_chunk)
        f = jnp.maximum(_dot(hb, w1_ref[:, cols]) + b1_ref[:, cols], 0.0)
        part = _dot((f * f).astype(_BF16), w2_ref[cols, :])
        acc = part if acc is None else acc + part
    f2 = (acc + b2_ref[...]).reshape(tb, tl, D_MODEL)
    x2 = x3 + g2 * f2
    y_ref[...] = _rms(x2) * fw_ref[...]


def _ffn(x1, mod, wts, *, tb, tl):
    nb, seq, _ = x1.shape
    assert nb % tb == 0 and seq % tl == 0

    def const(shape):
        return pl.BlockSpec(shape, lambda b, s: (0,) * len(shape))

    return pl.pallas_call(
        functools.partial(_ffn_kernel, ff_chunk=1024),
        grid=(nb // tb, seq // tl),
        in_specs=[
            pl.BlockSpec((tb, tl, D_MODEL), lambda b, s: (b, s, 0)),
            pl.BlockSpec((tb, 6, D_MODEL), lambda b, s: (b, 0, 0)),
            const((1, D_MODEL)),
            const((D_MODEL, D_FF)),
            const((1, D_FF)),
            const((D_FF, D_MODEL)),
            const((1, D_MODEL)),
            const((1, D_MODEL)),
        ],
        out_specs=pl.BlockSpec((tb, tl, D_MODEL), lambda b, s: (b, s, 0)),
        out_shape=jax.ShapeDtypeStruct((nb, seq, D_MODEL), _F32),
        compiler_params=pltpu.CompilerParams(
            dimension_semantics=("arbitrary", "arbitrary"),
            vmem_limit_bytes=V7X_VMEM_LIMIT_BYTES),
        name="ffn",
    )(x1, mod, *wts)


def _layer_weights(l, norm_mix_w, norm_ffn_w, w_in, pool_w, pool_b, pool_scale, conv_w, conv_b,
                   dt_bias, a_log, d_skip, ssd_norm_w, w_out, w_ff1, b_ff1, w_ff2, b_ff2):
    mixer_w = (
        norm_mix_w[l].reshape(1, D_MODEL),
        w_in[l].astype(_BF16),
        pool_w[l].astype(_BF16),
        pool_b[l].reshape(1, POOL_WIDTH),
        pool_scale[l].reshape(1, POOL_WIDTH),
        conv_w[l],
        conv_b[l].reshape(1, CONV_DIM),
        dt_bias[l].reshape(1, SSD_HEADS),
        a_log[l].reshape(1, SSD_HEADS),
        jnp.repeat(d_skip[l], SSD_HEAD_DIM).reshape(1, SSD_WIDTH),
        ssd_norm_w[l].reshape(1, SSD_WIDTH),
        w_out[l].astype(_BF16),
    )
    ffn_w = (
        norm_ffn_w[l].reshape(1, D_MODEL),
        w_ff1[l].astype(_BF16),
        b_ff1[l].reshape(1, D_FF),
        w_ff2[l].astype(_BF16),
        b_ff2[l].reshape(1, D_MODEL),
    )
    return mixer_w, ffn_w


def _run(x_prompt, x_sample, cache_pool, cache_conv, state_ssm, c_prompt, c_sample,
         norm_mix_w, norm_ffn_w, w_ada, b_ada, w_in, pool_w, pool_b, pool_scale,
         conv_w, conv_b, dt_bias, a_log, d_skip, ssd_norm_w, w_out,
         w_ff1, b_ff1, w_ff2, b_ff2, final_norm_w, *, prompt_tl, ffn_tl, past_len):
    depth = w_in.shape[0]
    assert depth == 1, "the final norm is fused into the (single) layer's ffn call"
    nbp, seq_p, _ = x_prompt.shape
    nbs, seq_s, _ = x_sample.shape
    fw = final_norm_w.reshape(1, D_MODEL)
    c_all = jnp.concatenate([c_prompt, c_sample], axis=0)
    xp, xs = x_prompt, x_sample
    outs = {k: [] for k in ("pp", "cp", "sp", "ps", "cs", "ss")}
    for l in range(depth):
        mod = _ada(c_all, w_ada[l], b_ada[l]).reshape(nbp + nbs, 6, D_MODEL)
        mod_p, mod_s = mod[:nbp], mod[nbp:]
        mixer_w, ffn_w = _layer_weights(l, norm_mix_w, norm_ffn_w, w_in, pool_w, pool_b, pool_scale,
                                        conv_w, conv_b, dt_bias, a_log, d_skip, ssd_norm_w, w_out,
                                        w_ff1, b_ff1, w_ff2, b_ff2)
        zp = jnp.zeros((1, POOL_HIST, POOL_WIDTH), _F32)
        zc = jnp.zeros((1, CONV_HIST, CONV_DIM), _F32)
        zh = jnp.zeros((1, SSD_HEADS, SSD_HEAD_DIM, SSD_STATE), _F32)
        xp, npool, nconv, nssm = _mixer(xp, mod_p, zp, zc, zh, mixer_w, pos0=0, tl=prompt_tl)
        xp = _ffn(xp, mod_p, ffn_w + (fw,), tb=1, tl=ffn_tl)
        outs["pp"].append(npool); outs["cp"].append(nconv); outs["sp"].append(nssm)
        xs, npool, nconv, nssm = _mixer(xs, mod_s, cache_pool[l], cache_conv[l], state_ssm[l], mixer_w,
                                        pos0=past_len, tl=seq_s)
        xs = _ffn(xs, mod_s, ffn_w + (fw,), tb=nbs, tl=seq_s)
        outs["ps"].append(npool); outs["cs"].append(nconv); outs["ss"].append(nssm)
    st = lambda k: outs[k][0][None]
    return (xp, xs, st("pp"), st("cp"), st("sp"), st("ps"), st("cs"), st("ss"))


def kernel(x_prompt, x_sample, cache_pool, cache_conv, state_ssm, c_prompt, c_sample, norm_mix_w, norm_ffn_w, w_ada, b_ada, w_in, pool_w, pool_b, pool_scale, conv_w, conv_b, dt_bias, a_log, d_skip, ssd_norm_w, w_out, w_ff1, b_ff1, w_ff2, b_ff2, final_norm_w):
    return _run(x_prompt, x_sample, cache_pool, cache_conv, state_ssm, c_prompt, c_sample,
                norm_mix_w, norm_ffn_w, w_ada, b_ada, w_in, pool_w, pool_b, pool_scale,
                conv_w, conv_b, dt_bias, a_log, d_skip, ssd_norm_w, w_out,
                w_ff1, b_ff1, w_ff2, b_ff2, final_norm_w,
                prompt_tl=512, ffn_tl=1024, past_len=4096)
```
